```python
import jax, jax.numpy as jnp
from jax import lax
import numpy as np

D_MODEL = 1024
BATCH = 8
SEQ = 2048
DEPTH = 1
DEC_BATCH = 128
DEC_SEQ = 8
PAST_LEN = 16384
PAGE_SIZE = 128

N_META = 16
D_MIX = D_MODEL
C_A = D_MIX // 2
HEAD_A = 64
H_A = C_A // HEAD_A
C_B = D_MIX - C_A
CONV_W = 31
D_DECAY_LORA = 64
D_AAA_LORA = 64
D_GATE_LORA = 128
SHIFT_COLS = 3 * C_A + D_DECAY_LORA + D_AAA_LORA + D_GATE_LORA
IN_COLS = SHIFT_COLS + 2 * C_B
N_EXPERTS = 32
TOP_K = 4
D_FF = D_MODEL
SWIGLU_LIMIT = 7.0
SWIGLU_ALPHA = 1.702
MOE_BLOCK = 128
RMS_EPS = 1e-5
LN_EPS = 1e-5
GN_EPS = 64e-5

kernel_name = "rwkv7_conformer_conv_moe_hybrid_step"


def rms_norm(x, g):
    xf = x.astype(jnp.float32)
    return xf * lax.rsqrt(jnp.mean(xf * xf, axis=-1, keepdims=True) + RMS_EPS) * g.astype(jnp.float32)


def layer_norm(x, w, b):
    mu = jnp.mean(x, axis=-1, keepdims=True)
    var = jnp.mean(jnp.square(x - mu), axis=-1, keepdims=True)
    return (x - mu) * lax.rsqrt(var + LN_EPS) * w.astype(jnp.float32) + b.astype(jnp.float32)


def heads(a):
    return a.reshape(*a.shape[:-1], H_A, HEAD_A)


def wkv7_scan(s0, r, decay, k, v, kk, kka):
    def step(s, inp):
        r_t, w_t, k_t, v_t, kk_t, b_t = inp
        sa = jnp.einsum('bhvk,bhk->bhv', s, kk_t)
        s = s * w_t[:, :, None, :] - sa[..., None] * b_t[:, :, None, :] + v_t[..., None] * k_t[:, :, None, :]
        y = jnp.einsum('bhvk,bhk->bhv', s, r_t)
        return s, y
    xs = tuple(jnp.moveaxis(a, 1, 0) for a in (r, decay, k, v, kk, kka))
    s, ys = lax.scan(step, s0.astype(jnp.float32), xs)
    return jnp.moveaxis(ys, 0, 1), s


def rwkv7_mix(p_a, prev_row, wkv0, tshift_mu, decay_w0, decay_w2, iclr_a0, iclr_a2, gate_g2,
              k_k, k_a, r_k, lnx_w, lnx_b):
    p_prev = jnp.concatenate([prev_row[:, None, :].astype(p_a.dtype), p_a[:, :-1]], axis=1)
    xs = p_a + (p_prev - p_a) * tshift_mu
    r, k, v, w_lo, a_lo, g_lo = jnp.split(
        xs, [C_A, 2 * C_A, 3 * C_A, 3 * C_A + D_DECAY_LORA, 3 * C_A + D_DECAY_LORA + D_AAA_LORA], axis=-1)
    w = -jax.nn.softplus(-(decay_w0 + jnp.tanh(w_lo) @ decay_w2)) - 0.5
    decay = jnp.exp(-jnp.exp(w))
    a = jax.nn.sigmoid(iclr_a0 + a_lo @ iclr_a2)
    g = jax.nn.sigmoid(g_lo) @ gate_g2
    kk = heads(k * k_k)
    kk = kk / jnp.maximum(jnp.sqrt(jnp.sum(kk * kk, axis=-1, keepdims=True)), 1e-12)
    k = k * (1.0 + (a - 1.0) * k_a)
    rh, kh, vh, ah = heads(r), heads(k), heads(v), heads(a)
    y, s_new = wkv7_scan(wkv0, rh, heads(decay), kh, vh, kk, kk * ah)
    mu = jnp.mean(y, axis=-1, keepdims=True)
    var = jnp.mean(jnp.square(y - mu), axis=-1, keepdims=True)
    y = ((y - mu) * lax.rsqrt(var + GN_EPS)).reshape(*y.shape[:-2], C_A) * lnx_w + lnx_b
    bonus = jnp.sum(rh * kh * r_k, axis=-1, keepdims=True) * vh
    y = (y + bonus.reshape(*bonus.shape[:-2], C_A)) * g
    return y, s_new, p_a[:, -1]


def conv_mix(p_b, buf, glu_b, dw_weight, dw_bias, conv_ln_w, conv_ln_b):
    u = p_b + glu_b
    u = u[..., :C_B] * jax.nn.sigmoid(u[..., C_B:])
    ext = jnp.concatenate([buf.astype(u.dtype), u], axis=1)
    z = lax.conv_general_dilated(ext, dw_weight.astype(ext.dtype)[:, None, :], window_strides=(1,),
                                 padding='VALID', dimension_numbers=('NWC', 'WIO', 'NWC'),
                                 feature_group_count=C_B) + dw_bias
    z = layer_norm(z, conv_ln_w, conv_ln_b)
    z = z * jax.nn.sigmoid(z)
    return z, ext[:, -(CONV_W - 1):]


def moe_ffn(h, w_router, b_router, w_gate, b_gate, w_up, b_up, w_down, b_down):
    n = h.shape[0]
    logits = (h @ w_router + b_router).astype(jnp.float32)
    top_val, top_idx = lax.top_k(logits, TOP_K)
    gates = jax.nn.softmax(top_val, axis=-1)
    n_assign = n * TOP_K
    flat_e = top_idx.reshape(-1)
    order = jnp.argsort(flat_e)
    sorted_e = flat_e[order]
    counts = jnp.bincount(flat_e, length=N_EXPERTS)
    padded = (counts + MOE_BLOCK - 1) // MOE_BLOCK * MOE_BLOCK
    pad_end = jnp.cumsum(padded)
    pad_start = pad_end - padded
    start = jnp.cumsum(counts) - counts
    dest = pad_start[sorted_e] + jnp.arange(n_assign) - start[sorted_e]
    n_blocks = (n_assign + N_EXPERTS * (MOE_BLOCK - 1) + MOE_BLOCK - 1) // MOE_BLOCK
    tok = order // TOP_K
    row_token = jnp.full((n_blocks * MOE_BLOCK,), n, jnp.int32).at[dest].set(tok.astype(jnp.int32))
    block_expert = jnp.minimum(jnp.searchsorted(pad_end, jnp.arange(n_blocks) * MOE_BLOCK, side='right'),
                               N_EXPERTS - 1)
    h_pad = jnp.concatenate([h, jnp.zeros((1, h.shape[1]), h.dtype)], axis=0)
    xb = h_pad[row_token].reshape(n_blocks, MOE_BLOCK, h.shape[1])

    def expert_block(args):
        xe, e = args
        gt = jnp.minimum(xe @ w_gate[e] + b_gate[e], SWIGLU_LIMIT)
        up = jnp.clip(xe @ w_up[e] + b_up[e], -SWIGLU_LIMIT, SWIGLU_LIMIT)
        return ((up + 1.0) * gt * jax.nn.sigmoid(SWIGLU_ALPHA * gt)) @ w_down[e] + b_down[e]

    yb = lax.map(expert_block, (xb, block_expert)).reshape(n_blocks * MOE_BLOCK, -1)
    contrib = yb[dest] * gates.reshape(-1)[order][:, None]
    return jax.ops.segment_sum(contrib, tok, num_segments=n)


def layer(x, prev_row, wkv0, conv_buf, norm_mix, w_in, tshift_mu, decay_w0, decay_w2, iclr_a0, iclr_a2,
          gate_g2, k_k, k_a, r_k, lnx_w, lnx_b, glu_b, dw_weight, dw_bias, conv_ln_w, conv_ln_b, w_out,
          norm_ffn, w_router, b_router, w_gate, b_gate, w_up, b_up, w_down, b_down):
    h = rms_norm(x, norm_mix)
    p = h @ w_in
    y_a, wkv_new, row_new = rwkv7_mix(p[..., :SHIFT_COLS], prev_row, wkv0, tshift_mu, decay_w0, decay_w2,
                                      iclr_a0, iclr_a2, gate_g2, k_k, k_a, r_k, lnx_w, lnx_b)
    y_b, buf_new = conv_mix(p[..., SHIFT_COLS:], conv_buf, glu_b, dw_weight, dw_bias, conv_ln_w, conv_ln_b)
    x = x + jnp.concatenate([y_a, y_b], axis=-1) @ w_out
    h2 = rms_norm(x, norm_ffn)
    b, t, d = h2.shape
    x = x + moe_ffn(h2.reshape(b * t, d), w_router, b_router, w_gate, b_gate, w_up, b_up,
                    w_down, b_down).reshape(b, t, d)
    return x, wkv_new, row_new, buf_new


def setup_inputs(seed: int = 0) -> dict:
    key = jax.random.key(seed)
    ks = jax.random.split(key, 40)
    L = DEPTH

    def nrm(k, shape, s):
        return jax.random.normal(k, shape, jnp.float32) * s

    return {
        "x_prompt": nrm(ks[0], (BATCH, SEQ, D_MODEL), 1.0),
        "x_sample": nrm(ks[1], (DEC_BATCH, DEC_SEQ, D_MODEL), 1.0),
        "state_wkv": nrm(ks[2], (L, DEC_BATCH, H_A, HEAD_A, HEAD_A), 0.3),
        "state_shift": nrm(ks[3], (L, DEC_BATCH, SHIFT_COLS), 1.0),
        "state_conv": nrm(ks[4], (L, DEC_BATCH, CONV_W - 1, C_B), 0.5),
        "meta_tokens": nrm(ks[5], (N_META, D_MODEL), 1.0),
        "norm_mix": 1.0 + nrm(ks[6], (L, D_MODEL), 0.02),
        "w_in": nrm(ks[7], (L, D_MODEL, IN_COLS), D_MODEL ** -0.5),
        "tshift_mu": jax.random.uniform(ks[8], (L, SHIFT_COLS), jnp.float32),
        "decay_w0": jax.random.uniform(ks[9], (L, C_A), jnp.float32, minval=-5.0, maxval=0.0),
        "decay_w2": nrm(ks[10], (L, D_DECAY_LORA, C_A), 0.1),
        "iclr_a0": nrm(ks[11], (L, C_A), 0.1),
        "iclr_a2": nrm(ks[12], (L, D_AAA_LORA, C_A), D_AAA_LORA ** -0.5),
        "gate_g2": nrm(ks[13], (L, D_GATE_LORA, C_A), D_GATE_LORA ** -0.5),
        "k_k": 0.85 + nrm(ks[14], (L, C_A), 0.02),
        "k_a": 1.0 + nrm(ks[15], (L, C_A), 0.02),
        "r_k": nrm(ks[16], (L, H_A, HEAD_A), 0.1),
        "lnx_w": 1.0 + nrm(ks[17], (L, C_A), 0.02),
        "lnx_b": nrm(ks[18], (L, C_A), 0.01),
        "glu_b": nrm(ks[19], (L, 2 * C_B), 0.02),
        "dw_weight": nrm(ks[20], (L, CONV_W, C_B), CONV_W ** -0.5),
        "dw_bias": nrm(ks[21], (L, C_B), 0.02),
        "conv_ln_w": 1.0 + nrm(ks[22], (L, C_B), 0.02),
        "conv_ln_b": nrm(ks[23], (L, C_B), 0.01),
        "w_out": nrm(ks[24], (L, D_MIX, D_MODEL), D_MIX ** -0.5),
        "norm_ffn": 1.0 + nrm(ks[25], (L, D_MODEL), 0.02),
        "w_router": nrm(ks[26], (L, D_MODEL, N_EXPERTS), D_MODEL ** -0.5),
        "b_router": nrm(ks[27], (L, N_EXPERTS), 0.01),
        "w_gate": nrm(ks[28], (L, N_EXPERTS, D_MODEL, D_FF), D_MODEL ** -0.5),
        "b_gate": nrm(ks[29], (L, N_EXPERTS, D_FF), 0.02),
        "w_up": nrm(ks[30], (L, N_EXPERTS, D_MODEL, D_FF), D_MODEL ** -0.5),
        "b_up": nrm(ks[31], (L, N_EXPERTS, D_FF), 0.02),
        "w_down": nrm(ks[32], (L, N_EXPERTS, D_FF, D_MODEL), D_FF ** -0.5),
        "b_down": nrm(ks[33], (L, N_EXPERTS, D_MODEL), 0.02),
        "norm_final": 1.0 + nrm(ks[34], (D_MODEL,), 0.02),
    }


def reference(x_prompt, x_sample, state_wkv, state_shift, state_conv, meta_tokens, norm_mix, w_in,
              tshift_mu, decay_w0, decay_w2, iclr_a0, iclr_a2, gate_g2, k_k, k_a, r_k, lnx_w, lnx_b,
              glu_b, dw_weight, dw_bias, conv_ln_w, conv_ln_b, w_out, norm_ffn, w_router, b_router,
              w_gate, b_gate, w_up, b_up, w_down, b_down, norm_final):
    b_p = x_prompt.shape[0]
    meta = jnp.broadcast_to(meta_tokens.astype(x_prompt.dtype)[None], (b_p, N_META, D_MODEL))
    xp = jnp.concatenate([meta, x_prompt], axis=1)
    xs = x_sample
    zero_row = jnp.zeros((b_p, SHIFT_COLS), jnp.float32)
    zero_wkv = jnp.zeros((b_p, H_A, HEAD_A, HEAD_A), jnp.float32)
    zero_buf = jnp.zeros((b_p, CONV_W - 1, C_B), jnp.float32)
    wkv_p, shift_p, conv_p, wkv_s, shift_s, conv_s = [], [], [], [], [], []
    for l in range(DEPTH):
        lw = tuple(a[l] for a in (norm_mix, w_in, tshift_mu, decay_w0, decay_w2, iclr_a0, iclr_a2, gate_g2,
                                  k_k, k_a, r_k, lnx_w, lnx_b, glu_b, dw_weight, dw_bias, conv_ln_w,
                                  conv_ln_b, w_out, norm_ffn, w_router, b_router, w_gate, b_gate, w_up,
                                  b_up, w_down, b_down))
        xp, sw, sr, sc = layer(xp, zero_row, zero_wkv, zero_buf, *lw)
        wkv_p.append(sw); shift_p.append(sr); conv_p.append(sc)
        xs, sw, sr, sc = layer(xs, state_shift[l], state_wkv[l], state_conv[l], *lw)
        wkv_s.append(sw); shift_s.append(sr); conv_s.append(sc)
    y_prompt = rms_norm(xp, norm_final)[:, N_META:].astype(x_prompt.dtype)
    y_sample = rms_norm(xs, norm_final).astype(x_sample.dtype)
    return (y_prompt, y_sample, jnp.stack(wkv_p), jnp.stack(shift_p), jnp.stack(conv_p),
            jnp.stack(wkv_s), jnp.stack(shift_s), jnp.stack(conv_s))
```

```python
import functools

import jax
import jax.numpy as jnp
from jax import lax
from jax.experimental import pallas as pl
from jax.experimental.pallas import tpu as pltpu

F32 = jnp.float32
BF16 = jnp.bfloat16

D_MODEL = 1024
N_META = 16
C_A = 512
HEAD = 64
N_HEADS = C_A // HEAD
C_B = 512
CONV_W = 31
D_DECAY_LORA = 64
D_AAA_LORA = 64
D_GATE_LORA = 128
LORA_WA = D_DECAY_LORA + D_AAA_LORA
SHIFT_COLS = 3 * C_A + LORA_WA + D_GATE_LORA
N_EXPERTS = 32
TOP_K = 4
SWIGLU_LIMIT = 7.0
SWIGLU_ALPHA = 1.702
RMS_EPS = 1e-5
LN_EPS = 1e-5
GN_EPS = 64e-5

LANES = 128
CHUNK = 64
CONV_HALO = 32
ROW_TILE = 512
MOE_ROWS = 256
VMEM_LIMIT = 56 * 1024 * 1024


def _cparams(sem):
    return pltpu.CompilerParams(dimension_semantics=sem, vmem_limit_bytes=VMEM_LIMIT)


def _dot(a, b):
    return jnp.dot(a.astype(BF16), b.astype(BF16), preferred_element_type=F32)


def _dot_nt(a, b):
    return lax.dot_general(a.astype(BF16), b.astype(BF16), (((1,), (1,)), ((), ())),
                           preferred_element_type=F32)


def _dot_tn(a, b):
    return lax.dot_general(a.astype(BF16), b.astype(BF16), (((0,), (0,)), ((), ())),
                           preferred_element_type=F32)


def _split2(x):
    hi = x.astype(BF16)
    lo = (x - hi.astype(F32)).astype(BF16)
    return hi, lo


def _split3(x):
    hi = x.astype(BF16)
    r1 = x - hi.astype(F32)
    mid = r1.astype(BF16)
    lo = (r1 - mid.astype(F32)).astype(BF16)
    return hi, mid, lo


def _dot_exact_rhs(x, m_bf16):
    hi, mid, lo = _split3(x)
    d = functools.partial(jnp.dot, preferred_element_type=F32)
    return d(hi, m_bf16) + d(mid, m_bf16) + d(lo, m_bf16)


def _dot_exact_lhs(m_bf16, x):
    hi, mid, lo = _split3(x)
    d = functools.partial(jnp.dot, preferred_element_type=F32)
    return d(m_bf16, hi) + d(m_bf16, mid) + d(m_bf16, lo)


def _rms(x, g):
    return x * lax.rsqrt(jnp.mean(x * x, axis=-1, keepdims=True) + RMS_EPS) * g


def _sigmoid(x):
    return 1.0 / (1.0 + jnp.exp(-x))


def _head_ones():
    i = lax.broadcasted_iota(jnp.int32, (C_A, C_A), 0) // HEAD
    j = lax.broadcasted_iota(jnp.int32, (C_A, C_A), 1) // HEAD
    return (i == j).astype(BF16)


def _pick_tile(n, mult, cap):
    best = mult
    t = mult
    while t <= min(n, cap):
        if n % t == 0:
            best = t
        t += mult
    assert n % best == 0, (n, mult)
    return best


def _in_proj_kernel(x_ref, g_ref, wa_ref, wb_ref, pa_ref, pb_ref):
    h = _rms(x_ref[...], g_ref[...]).astype(BF16)
    pa_ref[...] = jnp.dot(h, wa_ref[...], preferred_element_type=F32)
    pb_ref[...] = jnp.dot(h, wb_ref[...], preferred_element_type=F32)


def _in_proj(x, g, w_a, w_b):
    n = x.shape[0]
    tm = _pick_tile(n, 8, ROW_TILE)
    return pl.pallas_call(
        _in_proj_kernel,
        out_shape=(jax.ShapeDtypeStruct((n, SHIFT_COLS), F32), jax.ShapeDtypeStruct((n, 2 * C_B), F32)),
        grid=(n // tm,),
        in_specs=[pl.BlockSpec((tm, D_MODEL), lambda i: (i, 0)),
                  pl.BlockSpec((1, D_MODEL), lambda i: (0, 0)),
                  pl.BlockSpec((D_MODEL, SHIFT_COLS), lambda i: (0, 0)),
                  pl.BlockSpec((D_MODEL, 2 * C_B), lambda i: (0, 0))],
        out_specs=(pl.BlockSpec((tm, SHIFT_COLS), lambda i: (i, 0)),
                   pl.BlockSpec((tm, 2 * C_B), lambda i: (i, 0))),
        compiler_params=_cparams(("parallel",)),
        name="in_proj",
    )(x, g, w_a, w_b)


def _prep_math(p, pprev, mu, w0, a0, w_wa, w_g, k_k, k_a, ones_h, outs):
    r_ref, k_ref, v_ref, kk_ref, b_ref, lw_ref, g_ref = outs
    xs = p + (pprev - p) * mu
    r = xs[:, 0:C_A]
    k = xs[:, C_A:2 * C_A]
    v = xs[:, 2 * C_A:3 * C_A]
    lo = xs[:, 3 * C_A:3 * C_A + LORA_WA]
    g_lo = xs[:, 3 * C_A + LORA_WA:]
    lane = lax.broadcasted_iota(jnp.int32, lo.shape, 1)
    lo = jnp.where(lane < D_DECAY_LORA, jnp.tanh(lo), lo)
    wa = _dot(lo, w_wa)
    z = -(w0 + wa[:, :C_A])
    w = -(jnp.maximum(z, 0.0) + jnp.log(1.0 + jnp.exp(-jnp.abs(z)))) - 0.5
    lw_ref[...] = -jnp.exp(w)
    a = _sigmoid(a0 + wa[:, C_A:])
    g_ref[...] = _dot(_sigmoid(g_lo), w_g)
    kk = k * k_k
    hi, lo2 = _split2(kk * kk)
    ss = jnp.dot(hi, ones_h, preferred_element_type=F32) + jnp.dot(lo2, ones_h, preferred_element_type=F32)
    kk = kk / jnp.maximum(jnp.sqrt(ss), 1e-12)
    r_ref[...] = r
    v_ref[...] = v
    k_ref[...] = k * (1.0 + (a - 1.0) * k_a)
    kk_ref[...] = kk
    b_ref[...] = kk * a


def _prep_prompt_kernel(p_ref, halo_ref, mu_ref, w0_ref, a0_ref, wwa_ref, wg_ref, kk_ref_w, ka_ref, *outs):
    p = p_ref[...]
    prev_last = jnp.where(pl.program_id(1) == 0, 0.0, halo_ref[7:8, :])
    rolled = pltpu.roll(p, 1, 0)
    row = lax.broadcasted_iota(jnp.int32, p.shape, 0)
    pprev = jnp.where(row == 0, prev_last, rolled)
    _prep_math(p, pprev, mu_ref[...], w0_ref[...], a0_ref[...], wwa_ref[...], wg_ref[...],
               kk_ref_w[...], ka_ref[...], _head_ones(), outs)


def _prep_sample_kernel(p_ref, shift_ref, mu_ref, w0_ref, a0_ref, wwa_ref, wg_ref, kk_ref_w, ka_ref,
                        *outs_and_scratch):
    outs = outs_and_scratch
    p = p_ref[...]
    sb = shift_ref.shape[0]
    t_len = p.shape[0] // sb
    first = jnp.broadcast_to(shift_ref[...][:, None, :], (sb, t_len, SHIFT_COLS)).reshape(p.shape)
    row = lax.broadcasted_iota(jnp.int32, p.shape, 0)
    pprev = jnp.where(row % t_len == 0, first, pltpu.roll(p, 1, 0))
    _prep_math(p, pprev, mu_ref[...], w0_ref[...], a0_ref[...], wwa_ref[...], wg_ref[...],
               kk_ref_w[...], ka_ref[...], _head_ones(), outs)


def _prep_weight_specs(nd):
    z = (lambda *_: (0, 0))
    del nd
    return [pl.BlockSpec((1, SHIFT_COLS), z), pl.BlockSpec((1, C_A), z), pl.BlockSpec((1, C_A), z),
            pl.BlockSpec((LORA_WA, 2 * C_A), z), pl.BlockSpec((D_GATE_LORA, C_A), z),
            pl.BlockSpec((1, C_A), z), pl.BlockSpec((1, C_A), z)]


def _rwkv_prep_prompt(p_a, row0, n_seq, t_len, wts):
    tm = _pick_tile(t_len, 8, 704)
    nt = t_len // tm
    assert row0 % tm == 0
    base = row0 // tm
    out = jax.ShapeDtypeStruct((n_seq * t_len, C_A), F32)
    ospec = pl.BlockSpec((tm, C_A), lambda b, c: (b * nt + c, 0))
    return pl.pallas_call(
        _prep_prompt_kernel,
        out_shape=(out,) * 7,
        grid=(n_seq, nt),
        in_specs=[pl.BlockSpec((tm, SHIFT_COLS), lambda b, c: (base + b * nt + c, 0)),
                  pl.BlockSpec((8, SHIFT_COLS),
                               lambda b, c: (jnp.maximum((base + b * nt + c) * (tm // 8) - 1, 0), 0)),
                  ] + _prep_weight_specs(2),
        out_specs=(ospec,) * 7,
        compiler_params=_cparams(("parallel", "parallel")),
        name="rwkv_prep_prompt",
    )(p_a, p_a, *wts)


def _rwkv_prep_sample(p_a, row0, n_seq, t_len, shift, wts):
    sb = _pick_tile(n_seq, 8, 64)
    tm = sb * t_len
    assert row0 % tm == 0 and t_len % 8 == 0
    base = row0 // tm
    out = jax.ShapeDtypeStruct((n_seq * t_len, C_A), F32)
    ospec = pl.BlockSpec((tm, C_A), lambda i: (i, 0))
    return pl.pallas_call(
        _prep_sample_kernel,
        out_shape=(out,) * 7,
        grid=(n_seq // sb,),
        in_specs=[pl.BlockSpec((tm, SHIFT_COLS), lambda i: (base + i, 0)),
                  pl.BlockSpec((sb, SHIFT_COLS), lambda i: (i, 0)),
                  ] + _prep_weight_specs(1),
        out_specs=(ospec,) * 7,
        compiler_params=_cparams(("parallel",)),
        name="rwkv_prep_sample",
    )(p_a, shift, *wts)


def _scan_kernel(has_s0, *refs):
    if has_s0:
        (r_ref, k_ref, v_ref, kk_ref, b_ref, lw_ref, g_ref, s0_ref,
         lnw_ref, lnb_ref, rk_ref, y_ref, s_ref) = refs
    else:
        (r_ref, k_ref, v_ref, kk_ref, b_ref, lw_ref, g_ref,
         lnw_ref, lnb_ref, rk_ref, y_ref, s_ref) = refs
        s0_ref = None
    n_seq, c_len, _ = r_ref.shape

    @pl.when(pl.program_id(1) == 0)
    def _():
        if has_s0:
            s_ref[...] = s0_ref[...]
        else:
            s_ref[...] = jnp.zeros_like(s_ref)

    ti = lax.broadcasted_iota(jnp.int32, (c_len, c_len), 0)
    si = lax.broadcasted_iota(jnp.int32, (c_len, c_len), 1)
    tri_incl = (si <= ti).astype(BF16)
    strict = (si < ti).astype(F32)
    incl = (si <= ti).astype(F32)
    incl_signed = jnp.concatenate([incl, -incl], axis=1)
    ones_h = _head_ones()
    lnw, lnb, rk = lnw_ref[...], lnb_ref[...], rk_ref[...]

    def gsum(x):
        hi, lo = _split2(x)
        return jnp.dot(hi, ones_h, preferred_element_type=F32) + jnp.dot(lo, ones_h, preferred_element_type=F32)

    def seq_body(s, carry):
        r, k, v = r_ref[s], k_ref[s], v_ref[s]
        kk, b, lw = kk_ref[s], b_ref[s], lw_ref[s]
        cum = _dot_exact_lhs(tri_incl, lw)
        tot = cum[c_len - 1:c_len, :]
        g_in = jnp.exp(cum)
        kt = kk * jnp.exp(cum - lw)
        rt = r * g_in
        g_inv = jnp.exp(-cum)
        kh = k * g_inv
        bh = b * g_inv
        g_end = jnp.exp(tot - cum)
        e_mat = jnp.concatenate([k * g_end, -(b * g_end)], axis=0).astype(BF16)
        l_mat = jnp.concatenate([kt, rt], axis=0).astype(BF16)
        kbh = jnp.concatenate([kh, bh], axis=0).astype(BF16)
        g_tot = jnp.exp(tot)
        ys = []
        for h in range(N_HEADS):
            hs = slice(h * HEAD, (h + 1) * HEAD)
            st = s_ref[s, h]
            l_h = l_mat[:, hs]
            pm = _dot_nt(l_h, st)
            a_vk = _dot_nt(l_h[:c_len], kbh[:c_len, hs]) * strict
            n_ub = _dot_nt(l_h[:c_len], kbh[c_len:, hs]) * strict
            b_m = _dot_nt(l_h[c_len:], kbh[:, hs]) * incl_signed
            v_h = v[:, hs]
            x = pm[:c_len] + _dot(a_vk, v_h)
            x = x - _dot(n_ub, x)
            m = 2
            pw = n_ub
            while m < c_len:
                pw = _dot(pw, pw)
                x = x + _dot(pw, x)
                m *= 2
            vu = jnp.concatenate([v_h, x], axis=0)
            ys.append(pm[c_len:] + _dot(b_m, vu))
            s_ref[s, h] = st * g_tot[:, hs] + _dot_tn(vu, e_mat[:, hs])
        y = jnp.concatenate(ys, axis=1)
        mu = gsum(y) * (1.0 / HEAD)
        d = y - mu
        var = gsum(d * d) * (1.0 / HEAD)
        yn = d * lax.rsqrt(var + GN_EPS) * lnw + lnb
        bonus = gsum(r * k * rk) * v
        y_ref[s] = (yn + bonus) * g_ref[s]
        return carry

    lax.fori_loop(0, n_seq, seq_body, 0)


def _wkv_scan(streams, s0, n_seq, t_len, c_len, sb, lnw, lnb, rk):
    nc = t_len // c_len
    streams = [a.reshape(n_seq, t_len, C_A) for a in streams]
    dspec = pl.BlockSpec((sb, c_len, C_A), lambda i, c: (i, c, 0))
    sspec = pl.BlockSpec((sb, N_HEADS, HEAD, HEAD), lambda i, c: (i, 0, 0, 0))
    wspec = pl.BlockSpec((1, C_A), lambda i, c: (0, 0))
    ins = list(streams) + ([s0] if s0 is not None else []) + [lnw, lnb, rk]
    in_specs = [dspec] * 7 + ([sspec] if s0 is not None else []) + [wspec] * 3
    y, s_new = pl.pallas_call(
        functools.partial(_scan_kernel, s0 is not None),
        out_shape=(jax.ShapeDtypeStruct((n_seq, t_len, C_A), F32),
                   jax.ShapeDtypeStruct((n_seq, N_HEADS, HEAD, HEAD), F32)),
        grid=(n_seq // sb, nc),
        in_specs=in_specs,
        out_specs=(dspec, sspec),
        compiler_params=_cparams(("parallel", "arbitrary")),
        name="wkv_scan_c%d" % c_len,
    )(*ins)
    return y.reshape(n_seq * t_len, C_A), s_new


def _conv_tail(z, dwb, lnw, lnb):
    z = z + dwb
    mu = jnp.mean(z, axis=-1, keepdims=True)
    d = z - mu
    var = jnp.mean(d * d, axis=-1, keepdims=True)
    z = d * lax.rsqrt(var + LN_EPS) * lnw + lnb
    return z * _sigmoid(z)


def _glu(pb, glu_b):
    u = pb + glu_b
    return u[:, :C_B] * _sigmoid(u[:, C_B:])


def _conv_prompt_kernel(n_pad, pb_ref, halo_ref, glub_ref, dw_ref, dwb_ref, lnw_ref, lnb_ref,
                        y_ref, st_ref, ext_ref):
    c = pl.program_id(1)
    tm = pb_ref.shape[0]
    glub = glub_ref[...]
    halo = jnp.where(c == 0, 0.0, _glu(halo_ref[...], glub))
    u = _glu(pb_ref[...], glub)
    t_glob = c * tm + lax.broadcasted_iota(jnp.int32, u.shape, 0)
    u = jnp.where(t_glob < n_pad, 0.0, u)
    ext_ref[0:CONV_HALO, :] = halo
    ext_ref[CONV_HALO:, :] = u
    off = CONV_HALO - (CONV_W - 1)
    rb = 32

    def blk(i, carry):
        base = pl.multiple_of(i * rb, rb)
        win = ext_ref[pl.ds(base, rb + CONV_HALO), :]
        acc = jnp.zeros((rb, C_B), F32)
        for j in range(CONV_W):
            acc = acc + win[off + j:off + j + rb] * dw_ref[j:j + 1, :]
        y_ref[pl.ds(base, rb), :] = _conv_tail(acc, dwb_ref[...], lnw_ref[...], lnb_ref[...])
        return carry

    lax.fori_loop(0, tm // rb, blk, 0)

    @pl.when(c == pl.num_programs(1) - 1)
    def _():
        st_ref[0] = ext_ref[CONV_HALO + tm - (CONV_W - 1):CONV_HALO + tm, :]


def _conv_prompt(p_b, row0, n_seq, t_len, n_pad, wts):
    tm = _pick_tile(t_len, CONV_HALO, 704)
    nt = t_len // tm
    assert row0 % tm == 0
    base = row0 // tm
    z2 = lambda b, c: (0, 0)
    return pl.pallas_call(
        functools.partial(_conv_prompt_kernel, n_pad),
        out_shape=(jax.ShapeDtypeStruct((n_seq * t_len, C_B), F32),
                   jax.ShapeDtypeStruct((n_seq, CONV_W - 1, C_B), F32)),
        grid=(n_seq, nt),
        in_specs=[pl.BlockSpec((tm, 2 * C_B), lambda b, c: (base + b * nt + c, 0)),
                  pl.BlockSpec((CONV_HALO, 2 * C_B),
                               lambda b, c: (jnp.maximum((base + b * nt + c) * (tm // CONV_HALO) - 1, 0), 0)),
                  pl.BlockSpec((1, 2 * C_B), z2), pl.BlockSpec((CONV_W, C_B), z2),
                  pl.BlockSpec((1, C_B), z2), pl.BlockSpec((1, C_B), z2), pl.BlockSpec((1, C_B), z2)],
        out_specs=(pl.BlockSpec((tm, C_B), lambda b, c: (b * nt + c, 0)),
                   pl.BlockSpec((1, CONV_W - 1, C_B), lambda b, c: (b, 0, 0))),
        scratch_shapes=[pltpu.VMEM((CONV_HALO + tm, C_B), F32)],
        compiler_params=_cparams(("parallel", "arbitrary")),
        name="conv_prompt",
    )(p_b, p_b, *wts)


def _conv_sample_kernel(pb_ref, buf_ref, glub_ref, dw_ref, dwb_ref, lnw_ref, lnb_ref,
                        y_ref, st_ref, ext_ref):
    sb, t_len = buf_ref.shape[0], pb_ref.shape[0] // buf_ref.shape[0]
    hist = CONV_W - 1
    u = _glu(pb_ref[...], glub_ref[...])
    lead = ext_ref.shape[1] - hist - t_len
    ext_ref[:, lead:lead + hist, :] = buf_ref[...]
    ext_ref[:, lead + hist:, :] = u.reshape(sb, t_len, C_B)
    acc = jnp.zeros((sb, t_len, C_B), F32)
    for j in range(CONV_W):
        acc = acc + ext_ref[:, lead + j:lead + j + t_len, :] * dw_ref[j:j + 1, :]
    z = _conv_tail(acc.reshape(sb * t_len, C_B), dwb_ref[...], lnw_ref[...], lnb_ref[...])
    y_ref[...] = z
    st_ref[...] = ext_ref[:, lead + t_len:, :]


def _conv_sample(p_b, row0, n_seq, t_len, buf, wts):
    sb = _pick_tile(n_seq, 8, 32)
    tm = sb * t_len
    assert row0 % tm == 0 and t_len % 8 == 0
    base = row0 // tm
    hist = CONV_W - 1
    ext_rows = -(-(hist + t_len) // 8) * 8
    z1 = lambda i: (0, 0)
    return pl.pallas_call(
        _conv_sample_kernel,
        out_shape=(jax.ShapeDtypeStruct((n_seq * t_len, C_B), F32),
                   jax.ShapeDtypeStruct((n_seq, hist, C_B), F32)),
        grid=(n_seq // sb,),
        in_specs=[pl.BlockSpec((tm, 2 * C_B), lambda i: (base + i, 0)),
                  pl.BlockSpec((sb, hist, C_B), lambda i: (i, 0, 0)),
                  pl.BlockSpec((1, 2 * C_B), z1), pl.BlockSpec((CONV_W, C_B), z1),
                  pl.BlockSpec((1, C_B), z1), pl.BlockSpec((1, C_B), z1), pl.BlockSpec((1, C_B), z1)],
        out_specs=(pl.BlockSpec((tm, C_B), lambda i: (i, 0)),
                   pl.BlockSpec((sb, hist, C_B), lambda i: (i, 0, 0))),
        scratch_shapes=[pltpu.VMEM((sb, ext_rows, C_B), F32)],
        compiler_params=_cparams(("parallel",)),
        name="conv_sample",
    )(p_b, buf, *wts)


def _out_route_kernel(ya_ref, yb_ref, x_ref, woa_ref, wob_ref, g_ref, wrh_ref, wrl_ref, br_ref,
                      x1_ref, h2_ref, idx_ref, gate_ref):
    x1 = x_ref[...] + _dot(ya_ref[...], woa_ref[...]) + _dot(yb_ref[...], wob_ref[...])
    x1_ref[...] = x1
    h2 = _rms(x1, g_ref[...])
    h2_ref[...] = h2
    hi, lo = _split2(h2)
    d = functools.partial(jnp.dot, preferred_element_type=F32)
    logits = d(hi, wrh_ref[...]) + d(hi, wrl_ref[...]) + d(lo, wrh_ref[...]) + br_ref[...]
    lane = lax.broadcasted_iota(jnp.int32, logits.shape, 1)
    logits = jnp.where(lane < N_EXPERTS, logits, -jnp.inf)
    idx_out = jnp.zeros(logits.shape, jnp.int32)
    val_out = jnp.full(logits.shape, -jnp.inf, F32)
    for kk in range(TOP_K):
        m = jnp.max(logits, axis=-1, keepdims=True)
        sel = jnp.min(jnp.where(logits == m, lane, LANES), axis=-1, keepdims=True)
        idx_out = jnp.where(lane == kk, sel, idx_out)
        val_out = jnp.where(lane == kk, m, val_out)
        logits = jnp.where(lane == sel, -jnp.inf, logits)
    e = jnp.exp(val_out - jnp.max(val_out, axis=-1, keepdims=True))
    idx_ref[...] = idx_out
    gate_ref[...] = e / jnp.sum(e, axis=-1, keepdims=True)


def _out_route(y_a, y_b, x, wo_a, wo_b, g, wr_hi, wr_lo, b_r):
    n = x.shape[0]
    tm = _pick_tile(n, 8, ROW_TILE)
    z = lambda i: (0, 0)
    row = lambda w: pl.BlockSpec((tm, w), lambda i: (i, 0))
    return pl.pallas_call(
        _out_route_kernel,
        out_shape=(jax.ShapeDtypeStruct((n, D_MODEL), F32), jax.ShapeDtypeStruct((n, D_MODEL), F32),
                   jax.ShapeDtypeStruct((n, LANES), jnp.int32), jax.ShapeDtypeStruct((n, LANES), F32)),
        grid=(n // tm,),
        in_specs=[row(C_A), row(C_B), row(D_MODEL),
                  pl.BlockSpec((C_A, D_MODEL), z), pl.BlockSpec((C_B, D_MODEL), z),
                  pl.BlockSpec((1, D_MODEL), z),
                  pl.BlockSpec((D_MODEL, LANES), z), pl.BlockSpec((D_MODEL, LANES), z),
                  pl.BlockSpec((1, LANES), z)],
        out_specs=(row(D_MODEL), row(D_MODEL), row(LANES), row(LANES)),
        compiler_params=_cparams(("parallel",)),
        name="out_route",
    )(y_a, y_b, x, wo_a, wo_b, g, wr_hi, wr_lo, b_r)


def _gather_copy(src_hbm, dst_vmem, sem, src_row, dst_row):
    return pltpu.make_async_copy(src_hbm.at[pl.ds(src_row, 1), :], dst_vmem.at[pl.ds(dst_row, 1), :], sem)


def _ffn_kernel(tok_ref, bexp_ref, nblk_ref, h2_hbm, wg_ref, bg_ref, wu_ref, bu_ref, wd_ref, bd_ref,
                out_ref, xbuf, wg_bf, wu_bf, wd_bf, sems):
    j = pl.program_id(0)
    n_used = nblk_ref[0]
    rows = out_ref.shape[0]

    def issue(blk, slot):
        def body(i, carry):
            _gather_copy(h2_hbm, xbuf.at[slot], sems.at[slot], tok_ref[blk * rows + i], i).start()
            return carry
        lax.fori_loop(0, rows, body, 0)

    def drain(slot):
        def body(i, carry):
            _gather_copy(h2_hbm, xbuf.at[slot], sems.at[slot], 0, i).wait()
            return carry
        lax.fori_loop(0, rows, body, 0)

    @pl.when(jnp.logical_and(j == 0, n_used > 0))
    def _():
        issue(0, 0)

    @pl.when(j < n_used)
    def _():
        slot = lax.rem(j, 2)

        @pl.when(j + 1 < n_used)
        def _():
            issue(j + 1, 1 - slot)

        new_expert = jnp.logical_or(j == 0, bexp_ref[j] != bexp_ref[jnp.maximum(j - 1, 0)])

        @pl.when(new_expert)
        def _():
            wg_bf[...] = wg_ref[0].astype(BF16)
            wu_bf[...] = wu_ref[0].astype(BF16)
            wd_bf[...] = wd_ref[0].astype(BF16)

        drain(slot)
        x = xbuf[slot].astype(BF16)
        gt = jnp.minimum(jnp.dot(x, wg_bf[...], preferred_element_type=F32) + bg_ref[0], SWIGLU_LIMIT)
        up = jnp.clip(jnp.dot(x, wu_bf[...], preferred_element_type=F32) + bu_ref[0], -SWIGLU_LIMIT, SWIGLU_LIMIT)
        act = (up + 1.0) * gt * _sigmoid(SWIGLU_ALPHA * gt)
        out_ref[...] = jnp.dot(act.astype(BF16), wd_bf[...], preferred_element_type=F32) + bd_ref[0]

    @pl.when(j >= n_used)
    def _():
        out_ref[...] = jnp.zeros_like(out_ref)


def _moe_ffn(row_token, block_expert, n_used, h2, w_gate, b_gate, w_up, b_up, w_down, b_down):
    n_blocks = block_expert.shape[0]
    d_ff = w_gate.shape[2]
    wspec = lambda shape: pl.BlockSpec((1,) + shape, lambda j, tok, be, nb: (be[j], 0, 0))
    return pl.pallas_call(
        _ffn_kernel,
        out_shape=jax.ShapeDtypeStruct((n_blocks * MOE_ROWS, D_MODEL), F32),
        grid_spec=pltpu.PrefetchScalarGridSpec(
            num_scalar_prefetch=3,
            grid=(n_blocks,),
            in_specs=[pl.BlockSpec(memory_space=pl.ANY),
                      wspec((D_MODEL, d_ff)), wspec((1, d_ff)),
                      wspec((D_MODEL, d_ff)), wspec((1, d_ff)),
                      wspec((d_ff, D_MODEL)), wspec((1, D_MODEL))],
            out_specs=pl.BlockSpec((MOE_ROWS, D_MODEL), lambda j, tok, be, nb: (j, 0)),
            scratch_shapes=[pltpu.VMEM((2, MOE_ROWS, D_MODEL), F32),
                            pltpu.VMEM((D_MODEL, d_ff), BF16), pltpu.VMEM((D_MODEL, d_ff), BF16),
                            pltpu.VMEM((d_ff, D_MODEL), BF16),
                            pltpu.SemaphoreType.DMA((2,))]),
        compiler_params=_cparams(("arbitrary",)),
        name="moe_ffn",
    )(row_token, block_expert, n_used, h2, w_gate, b_gate.reshape(N_EXPERTS, 1, d_ff),
      w_up, b_up.reshape(N_EXPERTS, 1, d_ff), w_down, b_down.reshape(N_EXPERTS, 1, D_MODEL))


def _combine_kernel(dest_ref, yb_hbm, x1_ref, gate_ref, g_ref, y_ref, buf, sems):
    i = pl.program_id(0)
    n = pl.num_programs(0)
    tm = x1_ref.shape[0]

    def issue(tile, slot):
        def body(t, carry):
            for kk in range(TOP_K):
                _gather_copy(yb_hbm, buf.at[slot, kk], sems.at[slot],
                             dest_ref[(tile * tm + t) * TOP_K + kk], t).start()
            return carry
        lax.fori_loop(0, tm, body, 0)

    def drain(slot):
        def body(t, carry):
            for kk in range(TOP_K):
                _gather_copy(yb_hbm, buf.at[slot, kk], sems.at[slot], 0, t).wait()
            return carry
        lax.fori_loop(0, tm, body, 0)

    @pl.when(i == 0)
    def _():
        issue(0, 0)

    slot = lax.rem(i, 2)

    @pl.when(i + 1 < n)
    def _():
        issue(i + 1, 1 - slot)

    drain(slot)
    gates = gate_ref[...]
    acc = x1_ref[...]
    for kk in range(TOP_K):
        acc = acc + buf[slot, kk] * gates[:, kk:kk + 1]
    y_ref[...] = _rms(acc, g_ref[...])


def _combine(dest, yb, x1, gates, g_final):
    n = x1.shape[0]
    tm = _pick_tile(n, 8, 256)
    return pl.pallas_call(
        _combine_kernel,
        out_shape=jax.ShapeDtypeStruct((n, D_MODEL), F32),
        grid_spec=pltpu.PrefetchScalarGridSpec(
            num_scalar_prefetch=1,
            grid=(n // tm,),
            in_specs=[pl.BlockSpec(memory_space=pl.ANY),
                      pl.BlockSpec((tm, D_MODEL), lambda i, d: (i, 0)),
                      pl.BlockSpec((tm, LANES), lambda i, d: (i, 0)),
                      pl.BlockSpec((1, D_MODEL), lambda i, d: (0, 0))],
            out_specs=pl.BlockSpec((tm, D_MODEL), lambda i, d: (i, 0)),
            scratch_shapes=[pltpu.VMEM((2, TOP_K, tm, D_MODEL), F32),
                            pltpu.SemaphoreType.DMA((2,))]),
        compiler_params=_cparams(("arbitrary",)),
        name="moe_combine",
    )(dest, yb, x1, gates, g_final)


def _routing(idx):
    n = idx.shape[0]
    n_assign = n * TOP_K
    flat_e = idx[:, :TOP_K].reshape(-1)
    onehot = (flat_e[:, None] == jnp.arange(N_EXPERTS, dtype=jnp.int32)[None, :]).astype(jnp.int32)
    rank = jnp.take_along_axis(jnp.cumsum(onehot, axis=0), flat_e[:, None], axis=1)[:, 0] - 1
    counts = jnp.sum(onehot, axis=0)
    padded = (counts + MOE_ROWS - 1) // MOE_ROWS * MOE_ROWS
    pad_end = jnp.cumsum(padded)
    pad_start = pad_end - padded
    dest = (pad_start[flat_e] + rank).astype(jnp.int32)
    n_blocks = (n_assign + N_EXPERTS * (MOE_ROWS - 1) + MOE_ROWS - 1) // MOE_ROWS
    tok = jnp.arange(n_assign, dtype=jnp.int32) // TOP_K
    row_token = jnp.zeros((n_blocks * MOE_ROWS,), jnp.int32).at[dest].set(tok)
    n_used = (pad_end[-1] // MOE_ROWS).astype(jnp.int32)
    blk_start = jnp.arange(n_blocks, dtype=jnp.int32) * MOE_ROWS
    block_expert = jnp.minimum(jnp.searchsorted(pad_end, blk_start, side='right'), N_EXPERTS - 1)
    last_used = block_expert[jnp.maximum(n_used - 1, 0)]
    block_expert = jnp.where(jnp.arange(n_blocks) < n_used, block_expert, last_used).astype(jnp.int32)
    return dest, row_token, block_expert, n_used.reshape(1)


def _forward(x_prompt, x_sample, state_wkv, state_shift, state_conv, meta_tokens, norm_mix, w_in,
             tshift_mu, decay_w0, decay_w2, iclr_a0, iclr_a2, gate_g2, k_k, k_a, r_k, lnx_w, lnx_b,
             glu_b, dw_weight, dw_bias, conv_ln_w, conv_ln_b, w_out, norm_ffn, w_router, b_router,
             w_gate, b_gate, w_up, b_up, w_down, b_down, norm_final):
    n_p, seq, _ = x_prompt.shape
    n_s, t_s, _ = x_sample.shape
    t_real = N_META + seq
    n_pad = (-t_real) % CHUNK
    t_p = t_real + n_pad
    rows_p = n_p * t_p
    rows_s = n_s * t_s
    lyr = 0

    meta = jnp.broadcast_to(meta_tokens.astype(F32)[None], (n_p, N_META, D_MODEL))
    xp = jnp.concatenate([jnp.zeros((n_p, n_pad, D_MODEL), F32), meta, x_prompt], axis=1)
    x_all = jnp.concatenate([xp.reshape(rows_p, D_MODEL), x_sample.reshape(rows_s, D_MODEL)], axis=0)

    row = lambda a: a[lyr].reshape(1, -1).astype(F32)
    w_in_bf = w_in[lyr].astype(BF16)
    p_a, p_b = _in_proj(x_all, row(norm_mix), w_in_bf[:, :SHIFT_COLS], w_in_bf[:, SHIFT_COLS:])

    w_wa = jnp.zeros((LORA_WA, 2 * C_A), F32)
    w_wa = w_wa.at[:D_DECAY_LORA, :C_A].set(decay_w2[lyr]).at[D_DECAY_LORA:, C_A:].set(iclr_a2[lyr])
    prep_w = (row(tshift_mu), row(decay_w0), row(iclr_a0), w_wa.astype(BF16), gate_g2[lyr].astype(BF16),
              row(k_k), row(k_a))
    st_p = _rwkv_prep_prompt(p_a, 0, n_p, t_p, prep_w)
    st_s = _rwkv_prep_sample(p_a, rows_p, n_s, t_s, state_shift[lyr], prep_w)

    scan_w = (row(lnx_w), row(lnx_b), row(r_k))
    ya_p, wkv_p = _wkv_scan(st_p, None, n_p, t_p, CHUNK, 1, *scan_w)
    ya_s, wkv_s = _wkv_scan(st_s, state_wkv[lyr], n_s, t_s, t_s, _pick_tile(n_s, 1, 8), *scan_w)

    conv_w = (row(glu_b), dw_weight[lyr], row(dw_bias), row(conv_ln_w), row(conv_ln_b))
    yb_p, conv_p = _conv_prompt(p_b, 0, n_p, t_p, n_pad, conv_w)
    yb_s, conv_s = _conv_sample(p_b, rows_p, n_s, t_s, state_conv[lyr], conv_w)

    y_a = jnp.concatenate([ya_p, ya_s], axis=0)
    y_b = jnp.concatenate([yb_p, yb_s], axis=0)
    w_out_bf = w_out[lyr].astype(BF16)
    wr = jnp.zeros((D_MODEL, LANES), F32).at[:, :N_EXPERTS].set(w_router[lyr])
    wr_hi = wr.astype(BF16)
    wr_lo = (wr - wr_hi.astype(F32)).astype(BF16)
    b_r = jnp.zeros((1, LANES), F32).at[0, :N_EXPERTS].set(b_router[lyr])
    x1, h2, idx, gates = _out_route(y_a, y_b, x_all, w_out_bf[:C_A], w_out_bf[C_A:], row(norm_ffn),
                                    wr_hi, wr_lo, b_r)

    dest, row_token, block_expert, n_used = _routing(idx)
    yb = _moe_ffn(row_token, block_expert, n_used, h2, w_gate[lyr], b_gate[lyr], w_up[lyr], b_up[lyr],
                  w_down[lyr], b_down[lyr])
    y = _combine(dest, yb, x1, gates, norm_final.reshape(1, -1))

    y_prompt = y[:rows_p].reshape(n_p, t_p, D_MODEL)[:, n_pad + N_META:]
    y_sample = y[rows_p:].reshape(n_s, t_s, D_MODEL)
    shift_p = p_a[:rows_p].reshape(n_p, t_p, SHIFT_COLS)[:, -1]
    shift_s = p_a[rows_p:].reshape(n_s, t_s, SHIFT_COLS)[:, -1]
    return (y_prompt.astype(x_prompt.dtype), y_sample.astype(x_sample.dtype),
            wkv_p[None], shift_p[None], conv_p[None], wkv_s[None], shift_s[None], conv_s[None])


def kernel(x_prompt, x_sample, state_wkv, state_shift, state_conv, meta_tokens, norm_mix, w_in, tshift_mu, decay_w0, decay_w2, iclr_a0, iclr_a2, gate_g2, k_k, k_a, r_k, lnx_w, lnx_b, glu_b, dw_weight, dw_bias, conv_ln_w, conv_ln_b, w_out, norm_ffn, w_router, b_router, w_gate, b_gate, w_up, b_up, w_down, b_down, norm_final):
    assert w_in.shape[0] == 1, "single trunk layer"
    return _forward(x_prompt, x_sample, state_wkv, state_shift, state_conv, meta_tokens, norm_mix, w_in,
                    tshift_mu, decay_w0, decay_w2, iclr_a0, iclr_a2, gate_g2, k_k, k_a, r_k, lnx_w, lnx_b,
                    glu_b, dw_weight, dw_bias, conv_ln_w, conv_ln_b, w_out, norm_ffn, w_router, b_router,
                    w_gate, b_gate, w_up, b_up, w_down, b_down, norm_final)
```

```python
import functools

import jax
import jax.numpy as jnp
from jax import lax
from jax.experimental import pallas as pl
from jax.experimental.pallas import tpu as pltpu

F32 = jnp.float32
BF16 = jnp.bfloat16

D_MODEL = 1024
N_META = 16
C_A = 512
HEAD = 64
N_HEADS = C_A // HEAD
C_B = 512
CONV_W = 31
D_DECAY_LORA = 64
D_AAA_LORA = 64
D_GATE_LORA = 128
LORA_WA = D_DECAY_LORA + D_AAA_LORA
SHIFT_COLS = 3 * C_A + LORA_WA + D_GATE_LORA
N_EXPERTS = 32
TOP_K = 4
SWIGLU_LIMIT = 7.0
SWIGLU_ALPHA = 1.702
RMS_EPS = 1e-5
LN_EPS = 1e-5
GN_EPS = 64e-5

LANES = 128
CHUNK = 64
CONV_HALO = 32
ROW_TILE = 512
MOE_ROWS = 256
VMEM_LIMIT = 56 * 1024 * 1024


def _cparams(sem):
    return pltpu.CompilerParams(dimension_semantics=sem, vmem_limit_bytes=VMEM_LIMIT)


def _dot(a, b):
    return jnp.dot(a.astype(BF16), b.astype(BF16), preferred_element_type=F32)


def _dot_nt(a, b):
    return lax.dot_general(a.astype(BF16), b.astype(BF16), (((1,), (1,)), ((), ())),
                           preferred_element_type=F32)


def _dot_tn(a, b):
    return lax.dot_general(a.astype(BF16), b.astype(BF16), (((0,), (0,)), ((), ())),
                           preferred_element_type=F32)


def _split2(x):
    hi = x.astype(BF16)
    lo = (x - hi.astype(F32)).astype(BF16)
    return hi, lo


def _split3(x):
    hi = x.astype(BF16)
    r1 = x - hi.astype(F32)
    mid = r1.astype(BF16)
    lo = (r1 - mid.astype(F32)).astype(BF16)
    return hi, mid, lo


def _dot_exact_rhs(x, m_bf16):
    hi, mid, lo = _split3(x)
    d = functools.partial(jnp.dot, preferred_element_type=F32)
    return d(hi, m_bf16) + d(mid, m_bf16) + d(lo, m_bf16)


def _dot_exact_lhs(m_bf16, x):
    hi, mid, lo = _split3(x)
    d = functools.partial(jnp.dot, preferred_element_type=F32)
    return d(m_bf16, hi) + d(m_bf16, mid) + d(m_bf16, lo)


def _rms(x, g):
    return x * lax.rsqrt(jnp.mean(x * x, axis=-1, keepdims=True) + RMS_EPS) * g


def _sigmoid(x):
    return 1.0 / (1.0 + jnp.exp(-x))


def _head_ones():
    i = lax.broadcasted_iota(jnp.int32, (C_A, C_A), 0) // HEAD
    j = lax.broadcasted_iota(jnp.int32, (C_A, C_A), 1) // HEAD
    return (i == j).astype(BF16)


def _pick_tile(n, mult, cap):
    best = mult
    t = mult
    while t <= min(n, cap):
        if n % t == 0:
            best = t
        t += mult
    assert n % best == 0, (n, mult)
    return best


def _in_proj_kernel(x_ref, g_ref, wa_ref, wb_ref, pa_ref, pb_ref):
    h = _rms(x_ref[...], g_ref[...]).astype(BF16)
    pa_ref[...] = jnp.dot(h, wa_ref[...], preferred_element_type=F32)
    pb_ref[...] = jnp.dot(h, wb_ref[...], preferred_element_type=F32)


def _in_proj(x, g, w_a, w_b):
    n = x.shape[0]
    tm = _pick_tile(n, 8, ROW_TILE)
    return pl.pallas_call(
        _in_proj_kernel,
        out_shape=(jax.ShapeDtypeStruct((n, SHIFT_COLS), F32), jax.ShapeDtypeStruct((n, 2 * C_B), F32)),
        grid=(n // tm,),
        in_specs=[pl.BlockSpec((tm, D_MODEL), lambda i: (i, 0)),
                  pl.BlockSpec((1, D_MODEL), lambda i: (0, 0)),
                  pl.BlockSpec((D_MODEL, SHIFT_COLS), lambda i: (0, 0)),
                  pl.BlockSpec((D_MODEL, 2 * C_B), lambda i: (0, 0))],
        out_specs=(pl.BlockSpec((tm, SHIFT_COLS), lambda i: (i, 0)),
                   pl.BlockSpec((tm, 2 * C_B), lambda i: (i, 0))),
        compiler_params=_cparams(("parallel",)),
        name="in_proj",
    )(x, g, w_a, w_b)


def _prep_math(p, pprev, mu, w0, a0, w_wa, w_g, k_k, k_a, ones_h, outs):
    r_ref, k_ref, v_ref, kk_ref, b_ref, lw_ref, g_ref = outs
    xs = p + (pprev - p) * mu
    r = xs[:, 0:C_A]
    k = xs[:, C_A:2 * C_A]
    v = xs[:, 2 * C_A:3 * C_A]
    lo = xs[:, 3 * C_A:3 * C_A + LORA_WA]
    g_lo = xs[:, 3 * C_A + LORA_WA:]
    lane = lax.broadcasted_iota(jnp.int32, lo.shape, 1)
    lo = jnp.where(lane < D_DECAY_LORA, jnp.tanh(lo), lo)
    wa = _dot(lo, w_wa)
    z = -(w0 + wa[:, :C_A])
    w = -(jnp.maximum(z, 0.0) + jnp.log(1.0 + jnp.exp(-jnp.abs(z)))) - 0.5
    lw_ref[...] = -jnp.exp(w)
    a = _sigmoid(a0 + wa[:, C_A:])
    g_ref[...] = _dot(_sigmoid(g_lo), w_g)
    kk = k * k_k
    hi, lo2 = _split2(kk * kk)
    ss = jnp.dot(hi, ones_h, preferred_element_type=F32) + jnp.dot(lo2, ones_h, preferred_element_type=F32)
    kk = kk / jnp.maximum(jnp.sqrt(ss), 1e-12)
    r_ref[...] = r
    v_ref[...] = v
    k_ref[...] = k * (1.0 + (a - 1.0) * k_a)
    kk_ref[...] = kk
    b_ref[...] = kk * a


def _prep_prompt_kernel(p_ref, halo_ref, mu_ref, w0_ref, a0_ref, wwa_ref, wg_ref, kk_ref_w, ka_ref, *outs):
    p = p_ref[...]
    prev_last = jnp.where(pl.program_id(1) == 0, 0.0, halo_ref[7:8, :])
    rolled = pltpu.roll(p, 1, 0)
    row = lax.broadcasted_iota(jnp.int32, p.shape, 0)
    pprev = jnp.where(row == 0, prev_last, rolled)
    _prep_math(p, pprev, mu_ref[...], w0_ref[...], a0_ref[...], wwa_ref[...], wg_ref[...],
               kk_ref_w[...], ka_ref[...], _head_ones(), outs)


def _prep_sample_kernel(p_ref, shift_ref, mu_ref, w0_ref, a0_ref, wwa_ref, wg_ref, kk_ref_w, ka_ref,
                        *outs_and_scratch):
    outs = outs_and_scratch
    p = p_ref[...]
    sb = shift_ref.shape[0]
    t_len = p.shape[0] // sb
    first = jnp.broadcast_to(shift_ref[...][:, None, :], (sb, t_len, SHIFT_COLS)).reshape(p.shape)
    row = lax.broadcasted_iota(jnp.int32, p.shape, 0)
    pprev = jnp.where(row % t_len == 0, first, pltpu.roll(p, 1, 0))
    _prep_math(p, pprev, mu_ref[...], w0_ref[...], a0_ref[...], wwa_ref[...], wg_ref[...],
               kk_ref_w[...], ka_ref[...], _head_ones(), outs)


def _prep_weight_specs(nd):
    z = (lambda *_: (0, 0))
    del nd
    return [pl.BlockSpec((1, SHIFT_COLS), z), pl.BlockSpec((1, C_A), z), pl.BlockSpec((1, C_A), z),
            pl.BlockSpec((LORA_WA, 2 * C_A), z), pl.BlockSpec((D_GATE_LORA, C_A), z),
            pl.BlockSpec((1, C_A), z), pl.BlockSpec((1, C_A), z)]


def _rwkv_prep_prompt(p_a, row0, n_seq, t_len, wts):
    tm = _pick_tile(t_len, 8, 704)
    nt = t_len // tm
    assert row0 % tm == 0
    base = row0 // tm
    out = jax.ShapeDtypeStruct((n_seq * t_len, C_A), F32)
    ospec = pl.BlockSpec((tm, C_A), lambda b, c: (b * nt + c, 0))
    return pl.pallas_call(
        _prep_prompt_kernel,
        out_shape=(out,) * 7,
        grid=(n_seq, nt),
        in_specs=[pl.BlockSpec((tm, SHIFT_COLS), lambda b, c: (base + b * nt + c, 0)),
                  pl.BlockSpec((8, SHIFT_COLS),
                               lambda b, c: (jnp.maximum((base + b * nt + c) * (tm // 8) - 1, 0), 0)),
                  ] + _prep_weight_specs(2),
        out_specs=(ospec,) * 7,
        compiler_params=_cparams(("parallel", "parallel")),
        name="rwkv_prep_prompt",
    )(p_a, p_a, *wts)


def _rwkv_prep_sample(p_a, row0, n_seq, t_len, shift, wts):
    sb = _pick_tile(n_seq, 8, 64)
    tm = sb * t_len
    assert row0 % tm == 0 and t_len % 8 == 0
    base = row0 // tm
    out = jax.ShapeDtypeStruct((n_seq * t_len, C_A), F32)
    ospec = pl.BlockSpec((tm, C_A), lambda i: (i, 0))
    return pl.pallas_call(
        _prep_sample_kernel,
        out_shape=(out,) * 7,
        grid=(n_seq // sb,),
        in_specs=[pl.BlockSpec((tm, SHIFT_COLS), lambda i: (base + i, 0)),
                  pl.BlockSpec((sb, SHIFT_COLS), lambda i: (i, 0)),
                  ] + _prep_weight_specs(1),
        out_specs=(ospec,) * 7,
        compiler_params=_cparams(("parallel",)),
        name="rwkv_prep_sample",
    )(p_a, shift, *wts)


def _scan_kernel(has_s0, unroll, *refs):
    if has_s0:
        (r_ref, k_ref, v_ref, kk_ref, b_ref, lw_ref, g_ref, s0_ref,
         lnw_ref, lnb_ref, rk_ref, y_ref, s_ref) = refs
    else:
        (r_ref, k_ref, v_ref, kk_ref, b_ref, lw_ref, g_ref,
         lnw_ref, lnb_ref, rk_ref, y_ref, s_ref) = refs
        s0_ref = None
    n_seq, c_len, _ = r_ref.shape

    @pl.when(pl.program_id(1) == 0)
    def _():
        if has_s0:
            s_ref[...] = s0_ref[...]
        else:
            s_ref[...] = jnp.zeros_like(s_ref)

    ti = lax.broadcasted_iota(jnp.int32, (c_len, c_len), 0)
    si = lax.broadcasted_iota(jnp.int32, (c_len, c_len), 1)
    tri_incl = (si <= ti).astype(BF16)
    strict = (si < ti).astype(F32)
    incl = (si <= ti).astype(F32)
    incl_signed = jnp.concatenate([incl, -incl], axis=1)
    ones_h = _head_ones()
    lnw, lnb, rk = lnw_ref[...], lnb_ref[...], rk_ref[...]
    heads = range(N_HEADS)
    hsl = [slice(h * HEAD, (h + 1) * HEAD) for h in heads]

    def gsum(x):
        hi, lo = _split2(x)
        return jnp.dot(hi, ones_h, preferred_element_type=F32) + jnp.dot(lo, ones_h, preferred_element_type=F32)

    def load(s):
        return ([ref[s] for ref in (r_ref, k_ref, v_ref, kk_ref, b_ref, lw_ref, g_ref)],
                [s_ref[s, h] for h in heads])

    def compute(streams, states):
        r, k, v, kk, b, lw, g = streams
        cum = _dot_exact_lhs(tri_incl, lw)
        tot = cum[c_len - 1:c_len, :]
        g_inv = jnp.exp(-cum)
        g_end = jnp.exp(tot - cum)
        l_mat = jnp.concatenate([kk * jnp.exp(cum - lw), r * jnp.exp(cum)], axis=0).astype(BF16)
        kbh = jnp.concatenate([k * g_inv, b * g_inv], axis=0).astype(BF16)
        e_mat = jnp.concatenate([k * g_end, -(b * g_end)], axis=0).astype(BF16)
        g_tot = jnp.exp(tot)
        l_h = [l_mat[:, hs] for hs in hsl]
        v_h = [v[:, hs] for hs in hsl]
        n_ub = [_dot_nt(l_h[h][:c_len], kbh[c_len:, hsl[h]]) * strict for h in heads]
        a_vk = [_dot_nt(l_h[h][:c_len], kbh[:c_len, hsl[h]]) * strict for h in heads]
        pm = [_dot_nt(l_h[h], states[h]) for h in heads]
        x = [pm[h][:c_len] + _dot(a_vk[h], v_h[h]) for h in heads]
        x = [x[h] - _dot(n_ub[h], x[h]) for h in heads]
        pw = n_ub
        m = 2
        while m < c_len:
            pw = [_dot(p, p) for p in pw]
            x = [x[h] + _dot(pw[h], x[h]) for h in heads]
            m *= 2
        b_m = [_dot_nt(l_h[h][c_len:], kbh[:, hsl[h]]) * incl_signed for h in heads]
        vu = [jnp.concatenate([v_h[h], x[h]], axis=0) for h in heads]
        y = jnp.concatenate([pm[h][c_len:] + _dot(b_m[h], vu[h]) for h in heads], axis=1)
        new_states = [states[h] * g_tot[:, hsl[h]] + _dot_tn(vu[h], e_mat[:, hsl[h]]) for h in heads]
        mu = gsum(y) * (1.0 / HEAD)
        d = y - mu
        var = gsum(d * d) * (1.0 / HEAD)
        yn = d * lax.rsqrt(var + GN_EPS) * lnw + lnb
        bonus = gsum(r * k * rk) * v
        return (yn + bonus) * g, new_states

    def group(base):
        idx = [base + i for i in range(unroll)]
        loaded = [load(s) for s in idx]
        results = [compute(*d) for d in loaded]
        for s, (y, new_states) in zip(idx, results):
            y_ref[s] = y
            for h in heads:
                s_ref[s, h] = new_states[h]

    if n_seq == unroll:
        group(0)
    else:
        def body(i, carry):
            group(i * unroll)
            return carry
        lax.fori_loop(0, n_seq // unroll, body, 0)


def _wkv_scan(streams, s0, n_seq, t_len, c_len, sb, unroll, lnw, lnb, rk):
    nc = t_len // c_len
    streams = [a.reshape(n_seq, t_len, C_A) for a in streams]
    dspec = pl.BlockSpec((sb, c_len, C_A), lambda i, c: (i, c, 0))
    sspec = pl.BlockSpec((sb, N_HEADS, HEAD, HEAD), lambda i, c: (i, 0, 0, 0))
    wspec = pl.BlockSpec((1, C_A), lambda i, c: (0, 0))
    ins = list(streams) + ([s0] if s0 is not None else []) + [lnw, lnb, rk]
    in_specs = [dspec] * 7 + ([sspec] if s0 is not None else []) + [wspec] * 3
    y, s_new = pl.pallas_call(
        functools.partial(_scan_kernel, s0 is not None, unroll),
        out_shape=(jax.ShapeDtypeStruct((n_seq, t_len, C_A), F32),
                   jax.ShapeDtypeStruct((n_seq, N_HEADS, HEAD, HEAD), F32)),
        grid=(n_seq // sb, nc),
        in_specs=in_specs,
        out_specs=(dspec, sspec),
        compiler_params=_cparams(("parallel", "arbitrary")),
        name="wkv_scan_c%d" % c_len,
    )(*ins)
    return y.reshape(n_seq * t_len, C_A), s_new


def _conv_tail(z, dwb, lnw, lnb):
    z = z + dwb
    mu = jnp.mean(z, axis=-1, keepdims=True)
    d = z - mu
    var = jnp.mean(d * d, axis=-1, keepdims=True)
    z = d * lax.rsqrt(var + LN_EPS) * lnw + lnb
    return z * _sigmoid(z)


def _glu(pb, glu_b):
    u = pb + glu_b
    return u[:, :C_B] * _sigmoid(u[:, C_B:])


def _conv_prompt_kernel(n_pad, pb_ref, halo_ref, glub_ref, dw_ref, dwb_ref, lnw_ref, lnb_ref,
                        y_ref, st_ref, ext_ref):
    c = pl.program_id(1)
    tm = pb_ref.shape[0]
    glub = glub_ref[...]
    halo = jnp.where(c == 0, 0.0, _glu(halo_ref[...], glub))
    u = _glu(pb_ref[...], glub)
    t_glob = c * tm + lax.broadcasted_iota(jnp.int32, u.shape, 0)
    u = jnp.where(t_glob < n_pad, 0.0, u)
    ext_ref[0:CONV_HALO, :] = halo
    ext_ref[CONV_HALO:, :] = u
    off = CONV_HALO - (CONV_W - 1)
    rb = 32

    def blk(i, carry):
        base = pl.multiple_of(i * rb, rb)
        win = ext_ref[pl.ds(base, rb + CONV_HALO), :]
        acc = jnp.zeros((rb, C_B), F32)
        for j in range(CONV_W):
            acc = acc + win[off + j:off + j + rb] * dw_ref[j:j + 1, :]
        y_ref[pl.ds(base, rb), :] = _conv_tail(acc, dwb_ref[...], lnw_ref[...], lnb_ref[...])
        return carry

    lax.fori_loop(0, tm // rb, blk, 0)

    @pl.when(c == pl.num_programs(1) - 1)
    def _():
        st_ref[0] = ext_ref[CONV_HALO + tm - (CONV_W - 1):CONV_HALO + tm, :]


def _conv_prompt(p_b, row0, n_seq, t_len, n_pad, wts):
    tm = _pick_tile(t_len, CONV_HALO, 704)
    nt = t_len // tm
    assert row0 % tm == 0
    base = row0 // tm
    z2 = lambda b, c: (0, 0)
    return pl.pallas_call(
        functools.partial(_conv_prompt_kernel, n_pad),
        out_shape=(jax.ShapeDtypeStruct((n_seq * t_len, C_B), F32),
                   jax.ShapeDtypeStruct((n_seq, CONV_W - 1, C_B), F32)),
        grid=(n_seq, nt),
        in_specs=[pl.BlockSpec((tm, 2 * C_B), lambda b, c: (base + b * nt + c, 0)),
                  pl.BlockSpec((CONV_HALO, 2 * C_B),
                               lambda b, c: (jnp.maximum((base + b * nt + c) * (tm // CONV_HALO) - 1, 0), 0)),
                  pl.BlockSpec((1, 2 * C_B), z2), pl.BlockSpec((CONV_W, C_B), z2),
                  pl.BlockSpec((1, C_B), z2), pl.BlockSpec((1, C_B), z2), pl.BlockSpec((1, C_B), z2)],
        out_specs=(pl.BlockSpec((tm, C_B), lambda b, c: (b * nt + c, 0)),
                   pl.BlockSpec((1, CONV_W - 1, C_B), lambda b, c: (b, 0, 0))),
        scratch_shapes=[pltpu.VMEM((CONV_HALO + tm, C_B), F32)],
        compiler_params=_cparams(("parallel", "arbitrary")),
        name="conv_prompt",
    )(p_b, p_b, *wts)


def _conv_sample_kernel(pb_ref, buf_ref, glub_ref, dw_ref, dwb_ref, lnw_ref, lnb_ref,
                        y_ref, st_ref, ext_ref):
    sb, t_len = buf_ref.shape[0], pb_ref.shape[0] // buf_ref.shape[0]
    hist = CONV_W - 1
    u = _glu(pb_ref[...], glub_ref[...])
    lead = ext_ref.shape[1] - hist - t_len
    ext_ref[:, lead:lead + hist, :] = buf_ref[...]
    ext_ref[:, lead + hist:, :] = u.reshape(sb, t_len, C_B)
    acc = jnp.zeros((sb, t_len, C_B), F32)
    for j in range(CONV_W):
        acc = acc + ext_ref[:, lead + j:lead + j + t_len, :] * dw_ref[j:j + 1, :]
    z = _conv_tail(acc.reshape(sb * t_len, C_B), dwb_ref[...], lnw_ref[...], lnb_ref[...])
    y_ref[...] = z
    st_ref[...] = ext_ref[:, lead + t_len:, :]


def _conv_sample(p_b, row0, n_seq, t_len, buf, wts):
    sb = _pick_tile(n_seq, 8, 32)
    tm = sb * t_len
    assert row0 % tm == 0 and t_len % 8 == 0
    base = row0 // tm
    hist = CONV_W - 1
    ext_rows = -(-(hist + t_len) // 8) * 8
    z1 = lambda i: (0, 0)
    return pl.pallas_call(
        _conv_sample_kernel,
        out_shape=(jax.ShapeDtypeStruct((n_seq * t_len, C_B), F32),
                   jax.ShapeDtypeStruct((n_seq, hist, C_B), F32)),
        grid=(n_seq // sb,),
        in_specs=[pl.BlockSpec((tm, 2 * C_B), lambda i: (base + i, 0)),
                  pl.BlockSpec((sb, hist, C_B), lambda i: (i, 0, 0)),
                  pl.BlockSpec((1, 2 * C_B), z1), pl.BlockSpec((CONV_W, C_B), z1),
                  pl.BlockSpec((1, C_B), z1), pl.BlockSpec((1, C_B), z1), pl.BlockSpec((1, C_B), z1)],
        out_specs=(pl.BlockSpec((tm, C_B), lambda i: (i, 0)),
                   pl.BlockSpec((sb, hist, C_B), lambda i: (i, 0, 0))),
        scratch_shapes=[pltpu.VMEM((sb, ext_rows, C_B), F32)],
        compiler_params=_cparams(("parallel",)),
        name="conv_sample",
    )(p_b, buf, *wts)


TILE_SUB = D_MODEL // LANES


def _store_token_tiles(ref, x, lead=()):
    rows = x.shape[0]
    for c in range(TILE_SUB):
        ref[lead + (pl.ds(c, rows, stride=TILE_SUB), slice(None))] = x[:, c * LANES:(c + 1) * LANES]


def _load_token_tiles(ref, rows, lead=()):
    return jnp.concatenate([ref[lead + (pl.ds(c, rows, stride=TILE_SUB), slice(None))]
                            for c in range(TILE_SUB)], axis=1)


def _out_route_kernel(ya_ref, yb_ref, x_ref, woa_ref, wob_ref, g_ref, wrh_ref, wrl_ref, br_ref,
                      x1_ref, h2_ref, idx_ref, gate_ref):
    x1 = x_ref[...] + _dot(ya_ref[...], woa_ref[...]) + _dot(yb_ref[...], wob_ref[...])
    x1_ref[...] = x1
    h2 = _rms(x1, g_ref[...])
    _store_token_tiles(h2_ref, h2)
    hi, lo = _split2(h2)
    d = functools.partial(jnp.dot, preferred_element_type=F32)
    logits = d(hi, wrh_ref[...]) + d(hi, wrl_ref[...]) + d(lo, wrh_ref[...]) + br_ref[...]
    lane = lax.broadcasted_iota(jnp.int32, logits.shape, 1)
    logits = jnp.where(lane < N_EXPERTS, logits, -jnp.inf)
    idx_out = jnp.zeros(logits.shape, jnp.int32)
    val_out = jnp.full(logits.shape, -jnp.inf, F32)
    for kk in range(TOP_K):
        m = jnp.max(logits, axis=-1, keepdims=True)
        sel = jnp.min(jnp.where(logits == m, lane, LANES), axis=-1, keepdims=True)
        idx_out = jnp.where(lane == kk, sel, idx_out)
        val_out = jnp.where(lane == kk, m, val_out)
        logits = jnp.where(lane == sel, -jnp.inf, logits)
    e = jnp.exp(val_out - jnp.max(val_out, axis=-1, keepdims=True))
    idx_ref[...] = idx_out
    gate_ref[...] = e / jnp.sum(e, axis=-1, keepdims=True)


def _out_route(y_a, y_b, x, wo_a, wo_b, g, wr_hi, wr_lo, b_r):
    n = x.shape[0]
    tm = _pick_tile(n, 8, ROW_TILE)
    z = lambda i: (0, 0)
    row = lambda w: pl.BlockSpec((tm, w), lambda i: (i, 0))
    return pl.pallas_call(
        _out_route_kernel,
        out_shape=(jax.ShapeDtypeStruct((n, D_MODEL), F32), jax.ShapeDtypeStruct((n * TILE_SUB, LANES), F32),
                   jax.ShapeDtypeStruct((n, LANES), jnp.int32), jax.ShapeDtypeStruct((n, LANES), F32)),
        grid=(n // tm,),
        in_specs=[row(C_A), row(C_B), row(D_MODEL),
                  pl.BlockSpec((C_A, D_MODEL), z), pl.BlockSpec((C_B, D_MODEL), z),
                  pl.BlockSpec((1, D_MODEL), z),
                  pl.BlockSpec((D_MODEL, LANES), z), pl.BlockSpec((D_MODEL, LANES), z),
                  pl.BlockSpec((1, LANES), z)],
        out_specs=(row(D_MODEL), pl.BlockSpec((tm * TILE_SUB, LANES), lambda i: (i, 0)), row(LANES), row(LANES)),
        compiler_params=_cparams(("parallel",)),
        name="out_route",
    )(y_a, y_b, x, wo_a, wo_b, g, wr_hi, wr_lo, b_r)


def _ffn_kernel(n_tok, asg_ref, bexp_ref, nblk_ref, h2_hbm, wg_ref, bg_ref, wu_ref, bu_ref, wd_ref, bd_ref,
                out_hbm, xbuf0, xbuf1, ybuf0, ybuf1, wg_bf, wu_bf, wd_bf, gsem, ssem):
    j = pl.program_id(0)
    last = pl.num_programs(0) - 1
    n_used = nblk_ref[0]
    rows = MOE_ROWS
    xbufs, ybufs = (xbuf0, xbuf1), (ybuf0, ybuf1)

    def gather(i, tab_row, s):
        tok = jnp.minimum(lax.shift_right_logical(asg_ref[tab_row, i], jnp.int32(2)), n_tok - 1)
        return pltpu.make_async_copy(h2_hbm.at[pl.ds(pl.multiple_of(tok * TILE_SUB, TILE_SUB), TILE_SUB), :],
                                     xbufs[s].at[pl.ds(i * TILE_SUB, TILE_SUB), :], gsem.at[s])

    def scatter(i, tab_row, s):
        dst = asg_ref[tab_row, i]
        return pltpu.make_async_copy(ybufs[s].at[pl.ds(i * TILE_SUB, TILE_SUB), :],
                                     out_hbm.at[pl.ds(pl.multiple_of(dst * TILE_SUB, TILE_SUB), TILE_SUB), :],
                                     ssem.at[s])

    def start_all(fn, unrolled):
        if unrolled:
            for i in range(rows):
                fn(i).start()
        else:
            def body(i, carry):
                fn(i).start()
                return carry
            lax.fori_loop(0, rows, body, 0)

    def wait_gather(s):
        pltpu.make_async_copy(h2_hbm.at[pl.ds(0, rows * TILE_SUB), :], xbufs[s], gsem.at[s]).wait()

    def wait_scatter(s):
        pltpu.make_async_copy(ybufs[s], out_hbm.at[pl.ds(0, rows * TILE_SUB), :], ssem.at[s]).wait()

    @pl.when(j == 0)
    def _():
        ybuf0[...] = jnp.zeros_like(ybuf0)
        ybuf1[...] = jnp.zeros_like(ybuf1)
        start_all(lambda i: gather(i, 1, 0), False)

    def step(slot):
        other = 1 - slot

        def start_next(unrolled):
            start_all(lambda i: gather(i, j + 2, other), unrolled)
            start_all(lambda i: scatter(i, j, other), unrolled)

        @pl.when(j >= 1)
        def _():
            wait_scatter(slot)

        wait_gather(slot)

        @pl.when(j < n_used)
        def _():
            new_expert = jnp.logical_or(j == 0, bexp_ref[j] != bexp_ref[jnp.maximum(j - 1, 0)])

            @pl.when(new_expert)
            def _():
                wg_bf[...] = wg_ref[0].astype(BF16)
                wu_bf[...] = wu_ref[0].astype(BF16)
                wd_bf[...] = wd_ref[0].astype(BF16)

            x = _load_token_tiles(xbufs[slot], rows).astype(BF16)
            start_next(True)
            gt = jnp.minimum(jnp.dot(x, wg_bf[...], preferred_element_type=F32) + bg_ref[0], SWIGLU_LIMIT)
            up = jnp.clip(jnp.dot(x, wu_bf[...], preferred_element_type=F32) + bu_ref[0],
                          -SWIGLU_LIMIT, SWIGLU_LIMIT)
            act = (up + 1.0) * gt * _sigmoid(SWIGLU_ALPHA * gt)
            y = jnp.dot(act.astype(BF16), wd_bf[...], preferred_element_type=F32) + bd_ref[0]
            _store_token_tiles(ybufs[slot], y)

        @pl.when(j >= n_used)
        def _():
            start_next(False)

        @pl.when(j == last)
        def _():
            wait_gather(other)
            wait_scatter(other)

    for parity in (0, 1):
        pl.when(lax.rem(j, 2) == parity)(functools.partial(step, parity))


def _moe_ffn(asg, block_expert, n_used, h2t, w_gate, b_gate, w_up, b_up, w_down, b_down):
    n_steps = block_expert.shape[0]
    n_tok = h2t.shape[0] // TILE_SUB
    d_ff = w_gate.shape[2]
    out_rows = n_steps * MOE_ROWS
    wspec = lambda shape: pl.BlockSpec((1,) + shape, lambda j, a, be, nb: (be[j], 0, 0))
    tiles = (MOE_ROWS * TILE_SUB, LANES)
    return pl.pallas_call(
        functools.partial(_ffn_kernel, n_tok),
        out_shape=jax.ShapeDtypeStruct((out_rows * TILE_SUB, LANES), F32),
        grid_spec=pltpu.PrefetchScalarGridSpec(
            num_scalar_prefetch=3,
            grid=(n_steps,),
            in_specs=[pl.BlockSpec(memory_space=pl.ANY),
                      wspec((D_MODEL, d_ff)), wspec((1, d_ff)),
                      wspec((D_MODEL, d_ff)), wspec((1, d_ff)),
                      wspec((d_ff, D_MODEL)), wspec((1, D_MODEL))],
            out_specs=pl.BlockSpec(memory_space=pl.ANY),
            scratch_shapes=[pltpu.VMEM(tiles, F32), pltpu.VMEM(tiles, F32),
                            pltpu.VMEM(tiles, F32), pltpu.VMEM(tiles, F32),
                            pltpu.VMEM((D_MODEL, d_ff), BF16), pltpu.VMEM((D_MODEL, d_ff), BF16),
                            pltpu.VMEM((d_ff, D_MODEL), BF16),
                            pltpu.SemaphoreType.DMA((2,)), pltpu.SemaphoreType.DMA((2,))]),
        compiler_params=_cparams(("arbitrary",)),
        name="moe_ffn",
    )(asg, block_expert, n_used, h2t, w_gate, b_gate.reshape(N_EXPERTS, 1, d_ff),
      w_up, b_up.reshape(N_EXPERTS, 1, d_ff), w_down, b_down.reshape(N_EXPERTS, 1, D_MODEL))


def _combine_kernel(e_ref, x1_ref, gate_ref, g_ref, y_ref):
    tm = x1_ref.shape[0]
    gates = gate_ref[...]
    acc = x1_ref[...]
    for kk in range(TOP_K):
        yk = jnp.concatenate([e_ref[pl.ds(kk * TILE_SUB + c, tm, stride=TOP_K * TILE_SUB), :]
                              for c in range(TILE_SUB)], axis=1)
        acc = acc + yk * gates[:, kk:kk + 1]
    y_ref[...] = _rms(acc, g_ref[...])


def _combine(yt, x1, gates, g_final):
    n = x1.shape[0]
    tm = _pick_tile(n, 8, 256)
    return pl.pallas_call(
        _combine_kernel,
        out_shape=jax.ShapeDtypeStruct((n, D_MODEL), F32),
        grid=(n // tm,),
        in_specs=[pl.BlockSpec((tm * TOP_K * TILE_SUB, LANES), lambda i: (i, 0)),
                  pl.BlockSpec((tm, D_MODEL), lambda i: (i, 0)),
                  pl.BlockSpec((tm, LANES), lambda i: (i, 0)),
                  pl.BlockSpec((1, D_MODEL), lambda i: (0, 0))],
        out_specs=pl.BlockSpec((tm, D_MODEL), lambda i: (i, 0)),
        compiler_params=_cparams(("parallel",)),
        name="moe_combine",
    )(yt, x1, gates, g_final)


def _routing(idx):
    n = idx.shape[0]
    n_assign = n * TOP_K
    flat_e = idx[:, :TOP_K].reshape(-1)
    onehot = (flat_e[:, None] == jnp.arange(N_EXPERTS, dtype=jnp.int32)[None, :]).astype(jnp.int32)
    rank = jnp.take_along_axis(jnp.cumsum(onehot, axis=0), flat_e[:, None], axis=1)[:, 0] - 1
    counts = jnp.sum(onehot, axis=0)
    padded = (counts + MOE_ROWS - 1) // MOE_ROWS * MOE_ROWS
    pad_end = jnp.cumsum(padded)
    pad_start = pad_end - padded
    n_blocks = (n_assign + N_EXPERTS * (MOE_ROWS - 1) + MOE_ROWS - 1) // MOE_ROWS
    dest = (pad_start[flat_e] + rank).astype(jnp.int32) + MOE_ROWS
    q = jnp.arange((n_blocks + 3) * MOE_ROWS, dtype=jnp.int32) - MOE_ROWS
    e_of = jnp.sum((q[:, None] >= pad_end[None, :]).astype(jnp.int32), axis=1)
    waste = padded - counts
    waste_before = jnp.concatenate([jnp.zeros((1,), waste.dtype), jnp.cumsum(waste)])
    e_c = jnp.minimum(e_of, N_EXPERTS - 1)
    in_expert = MOE_ROWS + waste_before[e_c] + (q - pad_start[e_c] - counts[e_c])
    after = MOE_ROWS + waste_before[N_EXPERTS] + (q - pad_end[N_EXPERTS - 1])
    dump = jnp.where(q < 0, q + MOE_ROWS, jnp.where(e_of < N_EXPERTS, in_expert, after))
    asg = (n_assign + dump).astype(jnp.int32).at[dest].set(jnp.arange(n_assign, dtype=jnp.int32))
    n_used = (pad_end[-1] // MOE_ROWS).astype(jnp.int32)
    blk_start = jnp.arange(n_blocks + 1, dtype=jnp.int32) * MOE_ROWS
    block_expert = jnp.minimum(jnp.searchsorted(pad_end, blk_start, side='right'), N_EXPERTS - 1)
    last_used = block_expert[jnp.maximum(n_used - 1, 0)]
    block_expert = jnp.where(jnp.arange(n_blocks + 1) < n_used, block_expert, last_used).astype(jnp.int32)
    return asg.reshape(n_blocks + 3, MOE_ROWS), block_expert, n_used.reshape(1)


def _forward(x_prompt, x_sample, state_wkv, state_shift, state_conv, meta_tokens, norm_mix, w_in,
             tshift_mu, decay_w0, decay_w2, iclr_a0, iclr_a2, gate_g2, k_k, k_a, r_k, lnx_w, lnx_b,
             glu_b, dw_weight, dw_bias, conv_ln_w, conv_ln_b, w_out, norm_ffn, w_router, b_router,
             w_gate, b_gate, w_up, b_up, w_down, b_down, norm_final):
    n_p, seq, _ = x_prompt.shape
    n_s, t_s, _ = x_sample.shape
    t_real = N_META + seq
    n_pad = (-t_real) % CHUNK
    t_p = t_real + n_pad
    rows_p = n_p * t_p
    rows_s = n_s * t_s
    lyr = 0

    meta = jnp.broadcast_to(meta_tokens.astype(F32)[None], (n_p, N_META, D_MODEL))
    xp = jnp.concatenate([jnp.zeros((n_p, n_pad, D_MODEL), F32), meta, x_prompt], axis=1)
    x_all = jnp.concatenate([xp.reshape(rows_p, D_MODEL), x_sample.reshape(rows_s, D_MODEL)], axis=0)

    row = lambda a: a[lyr].reshape(1, -1).astype(F32)
    w_in_bf = w_in[lyr].astype(BF16)
    p_a, p_b = _in_proj(x_all, row(norm_mix), w_in_bf[:, :SHIFT_COLS], w_in_bf[:, SHIFT_COLS:])

    w_wa = jnp.zeros((LORA_WA, 2 * C_A), F32)
    w_wa = w_wa.at[:D_DECAY_LORA, :C_A].set(decay_w2[lyr]).at[D_DECAY_LORA:, C_A:].set(iclr_a2[lyr])
    prep_w = (row(tshift_mu), row(decay_w0), row(iclr_a0), w_wa.astype(BF16), gate_g2[lyr].astype(BF16),
              row(k_k), row(k_a))
    st_p = _rwkv_prep_prompt(p_a, 0, n_p, t_p, prep_w)
    st_s = _rwkv_prep_sample(p_a, rows_p, n_s, t_s, state_shift[lyr], prep_w)

    scan_w = (row(lnx_w), row(lnx_b), row(r_k))
    ya_p, wkv_p = _wkv_scan(st_p, None, n_p, t_p, CHUNK, 2, 2, *scan_w)
    ya_s, wkv_s = _wkv_scan(st_s, state_wkv[lyr], n_s, t_s, t_s, _pick_tile(n_s, 2, 16), 2, *scan_w)

    conv_w = (row(glu_b), dw_weight[lyr], row(dw_bias), row(conv_ln_w), row(conv_ln_b))
    yb_p, conv_p = _conv_prompt(p_b, 0, n_p, t_p, n_pad, conv_w)
    yb_s, conv_s = _conv_sample(p_b, rows_p, n_s, t_s, state_conv[lyr], conv_w)

    y_a = jnp.concatenate([ya_p, ya_s], axis=0)
    y_b = jnp.concatenate([yb_p, yb_s], axis=0)
    w_out_bf = w_out[lyr].astype(BF16)
    wr = jnp.zeros((D_MODEL, LANES), F32).at[:, :N_EXPERTS].set(w_router[lyr])
    wr_hi = wr.astype(BF16)
    wr_lo = (wr - wr_hi.astype(F32)).astype(BF16)
    b_r = jnp.zeros((1, LANES), F32).at[0, :N_EXPERTS].set(b_router[lyr])
    x1, h2, idx, gates = _out_route(y_a, y_b, x_all, w_out_bf[:C_A], w_out_bf[C_A:], row(norm_ffn),
                                    wr_hi, wr_lo, b_r)

    asg, block_expert, n_used = _routing(idx)
    yt = _moe_ffn(asg, block_expert, n_used, h2, w_gate[lyr], b_gate[lyr], w_up[lyr], b_up[lyr],
                  w_down[lyr], b_down[lyr])
    y = _combine(yt, x1, gates, norm_final.reshape(1, -1))

    y_prompt = y[:rows_p].reshape(n_p, t_p, D_MODEL)[:, n_pad + N_META:]
    y_sample = y[rows_p:].reshape(n_s, t_s, D_MODEL)
    shift_p = p_a[:rows_p].reshape(n_p, t_p, SHIFT_COLS)[:, -1]
    shift_s = p_a[rows_p:].reshape(n_s, t_s, SHIFT_COLS)[:, -1]
    return (y_prompt.astype(x_prompt.dtype), y_sample.astype(x_sample.dtype),
            wkv_p[None], shift_p[None], conv_p[None], wkv_s[None], shift_s[None], conv_s[None])


def kernel(x_prompt, x_sample, state_wkv, state_shift, state_conv, meta_tokens, norm_mix, w_in, tshift_mu, decay_w0, decay_w2, iclr_a0, iclr_a2, gate_g2, k_k, k_a, r_k, lnx_w, lnx_b, glu_b, dw_weight, dw_bias, conv_ln_w, conv_ln_b, w_out, norm_ffn, w_router, b_router, w_gate, b_gate, w_up, b_up, w_down, b_down, norm_final):
    assert w_in.shape[0] == 1, "single trunk layer"
    return _forward(x_prompt, x_sample, state_wkv, state_shift, state_conv, meta_tokens, norm_mix, w_in,
                    tshift_mu, decay_w0, decay_w2, iclr_a0, iclr_a2, gate_g2, k_k, k_a, r_k, lnx_w, lnx_b,
                    glu_b, dw_weight, dw_bias, conv_ln_w, conv_ln_b, w_out, norm_ffn, w_router, b_router,
                    w_gate, b_gate, w_up, b_up, w_down, b_down, norm_final)
```

```python
import functools

import jax
import jax.numpy as jnp
from jax import lax
from jax.experimental import pallas as pl
from jax.experimental.pallas import tpu as pltpu

F32 = jnp.float32
BF16 = jnp.bfloat16

D_MODEL = 1024
N_META = 16
C_A = 512
HEAD = 64
N_HEADS = C_A // HEAD
C_B = 512
CONV_W = 31
D_DECAY_LORA = 64
D_AAA_LORA = 64
D_GATE_LORA = 128
LORA_WA = D_DECAY_LORA + D_AAA_LORA
SHIFT_COLS = 3 * C_A + LORA_WA + D_GATE_LORA
N_EXPERTS = 32
TOP_K = 4
SWIGLU_LIMIT = 7.0
SWIGLU_ALPHA = 1.702
RMS_EPS = 1e-5
LN_EPS = 1e-5
GN_EPS = 64e-5

LANES = 128
CHUNK = 64
CONV_HALO = 32
ROW_TILE = 512
MOE_ROWS = 256
VMEM_LIMIT = 56 * 1024 * 1024


def _cparams(sem):
    return pltpu.CompilerParams(dimension_semantics=sem, vmem_limit_bytes=VMEM_LIMIT)


def _dot(a, b):
    return jnp.dot(a.astype(BF16), b.astype(BF16), preferred_element_type=F32)


def _dot_nt(a, b):
    return lax.dot_general(a.astype(BF16), b.astype(BF16), (((1,), (1,)), ((), ())),
                           preferred_element_type=F32)


def _dot_tn(a, b):
    return lax.dot_general(a.astype(BF16), b.astype(BF16), (((0,), (0,)), ((), ())),
                           preferred_element_type=F32)


def _split2(x):
    hi = x.astype(BF16)
    lo = (x - hi.astype(F32)).astype(BF16)
    return hi, lo


def _split3(x):
    hi = x.astype(BF16)
    r1 = x - hi.astype(F32)
    mid = r1.astype(BF16)
    lo = (r1 - mid.astype(F32)).astype(BF16)
    return hi, mid, lo


def _dot_exact_rhs(x, m_bf16):
    hi, mid, lo = _split3(x)
    d = functools.partial(jnp.dot, preferred_element_type=F32)
    return d(hi, m_bf16) + d(mid, m_bf16) + d(lo, m_bf16)


def _dot_exact_lhs(m_bf16, x):
    hi, mid, lo = _split3(x)
    d = functools.partial(jnp.dot, preferred_element_type=F32)
    return d(m_bf16, hi) + d(m_bf16, mid) + d(m_bf16, lo)


def _rms(x, g):
    return x * lax.rsqrt(jnp.mean(x * x, axis=-1, keepdims=True) + RMS_EPS) * g


def _sigmoid(x):
    return 1.0 / (1.0 + jnp.exp(-x))


def _head_ones():
    i = lax.broadcasted_iota(jnp.int32, (C_A, C_A), 0) // HEAD
    j = lax.broadcasted_iota(jnp.int32, (C_A, C_A), 1) // HEAD
    return (i == j).astype(BF16)


def _pick_tile(n, mult, cap):
    best = mult
    t = mult
    while t <= min(n, cap):
        if n % t == 0:
            best = t
        t += mult
    assert n % best == 0, (n, mult)
    return best


def _in_proj_kernel(x_ref, g_ref, wa_ref, wb_ref, pa_ref, pb_ref):
    h = _rms(x_ref[...], g_ref[...]).astype(BF16)
    pa_ref[...] = jnp.dot(h, wa_ref[...], preferred_element_type=F32)
    pb_ref[...] = jnp.dot(h, wb_ref[...], preferred_element_type=F32)


def _in_proj(x, g, w_a, w_b):
    n = x.shape[0]
    tm = _pick_tile(n, 8, ROW_TILE)
    return pl.pallas_call(
        _in_proj_kernel,
        out_shape=(jax.ShapeDtypeStruct((n, SHIFT_COLS), F32), jax.ShapeDtypeStruct((n, 2 * C_B), F32)),
        grid=(n // tm,),
        in_specs=[pl.BlockSpec((tm, D_MODEL), lambda i: (i, 0)),
                  pl.BlockSpec((1, D_MODEL), lambda i: (0, 0)),
                  pl.BlockSpec((D_MODEL, SHIFT_COLS), lambda i: (0, 0)),
                  pl.BlockSpec((D_MODEL, 2 * C_B), lambda i: (0, 0))],
        out_specs=(pl.BlockSpec((tm, SHIFT_COLS), lambda i: (i, 0)),
                   pl.BlockSpec((tm, 2 * C_B), lambda i: (i, 0))),
        compiler_params=_cparams(("parallel",)),
        name="in_proj",
    )(x, g, w_a, w_b)


def _prep_math(p, pprev, mu, w0, a0, w_wa, w_g, k_k, k_a, ones_h, outs):
    r_ref, k_ref, v_ref, kk_ref, b_ref, lw_ref, g_ref = outs
    xs = p + (pprev - p) * mu
    r = xs[:, 0:C_A]
    k = xs[:, C_A:2 * C_A]
    v = xs[:, 2 * C_A:3 * C_A]
    lo = xs[:, 3 * C_A:3 * C_A + LORA_WA]
    g_lo = xs[:, 3 * C_A + LORA_WA:]
    lane = lax.broadcasted_iota(jnp.int32, lo.shape, 1)
    lo = jnp.where(lane < D_DECAY_LORA, jnp.tanh(lo), lo)
    wa = _dot(lo, w_wa)
    z = -(w0 + wa[:, :C_A])
    w = -(jnp.maximum(z, 0.0) + jnp.log(1.0 + jnp.exp(-jnp.abs(z)))) - 0.5
    lw_ref[...] = -jnp.exp(w)
    a = _sigmoid(a0 + wa[:, C_A:])
    g_ref[...] = _dot(_sigmoid(g_lo), w_g)
    kk = k * k_k
    hi, lo2 = _split2(kk * kk)
    ss = jnp.dot(hi, ones_h, preferred_element_type=F32) + jnp.dot(lo2, ones_h, preferred_element_type=F32)
    kk = kk / jnp.maximum(jnp.sqrt(ss), 1e-12)
    r_ref[...] = r
    v_ref[...] = v
    k_ref[...] = k * (1.0 + (a - 1.0) * k_a)
    kk_ref[...] = kk
    b_ref[...] = kk * a


def _prep_prompt_kernel(p_ref, halo_ref, mu_ref, w0_ref, a0_ref, wwa_ref, wg_ref, kk_ref_w, ka_ref, *outs):
    p = p_ref[...]
    prev_last = jnp.where(pl.program_id(1) == 0, 0.0, halo_ref[7:8, :])
    rolled = pltpu.roll(p, 1, 0)
    row = lax.broadcasted_iota(jnp.int32, p.shape, 0)
    pprev = jnp.where(row == 0, prev_last, rolled)
    _prep_math(p, pprev, mu_ref[...], w0_ref[...], a0_ref[...], wwa_ref[...], wg_ref[...],
               kk_ref_w[...], ka_ref[...], _head_ones(), outs)


def _prep_sample_kernel(p_ref, shift_ref, mu_ref, w0_ref, a0_ref, wwa_ref, wg_ref, kk_ref_w, ka_ref,
                        *outs_and_scratch):
    outs = outs_and_scratch
    p = p_ref[...]
    sb = shift_ref.shape[0]
    t_len = p.shape[0] // sb
    first = jnp.broadcast_to(shift_ref[...][:, None, :], (sb, t_len, SHIFT_COLS)).reshape(p.shape)
    row = lax.broadcasted_iota(jnp.int32, p.shape, 0)
    pprev = jnp.where(row % t_len == 0, first, pltpu.roll(p, 1, 0))
    _prep_math(p, pprev, mu_ref[...], w0_ref[...], a0_ref[...], wwa_ref[...], wg_ref[...],
               kk_ref_w[...], ka_ref[...], _head_ones(), outs)


def _prep_weight_specs(nd):
    z = (lambda *_: (0, 0))
    del nd
    return [pl.BlockSpec((1, SHIFT_COLS), z), pl.BlockSpec((1, C_A), z), pl.BlockSpec((1, C_A), z),
            pl.BlockSpec((LORA_WA, 2 * C_A), z), pl.BlockSpec((D_GATE_LORA, C_A), z),
            pl.BlockSpec((1, C_A), z), pl.BlockSpec((1, C_A), z)]


def _rwkv_prep_prompt(p_a, row0, n_seq, t_len, wts):
    tm = _pick_tile(t_len, 8, 704)
    nt = t_len // tm
    assert row0 % tm == 0
    base = row0 // tm
    out = jax.ShapeDtypeStruct((n_seq * t_len, C_A), F32)
    ospec = pl.BlockSpec((tm, C_A), lambda b, c: (b * nt + c, 0))
    return pl.pallas_call(
        _prep_prompt_kernel,
        out_shape=(out,) * 7,
        grid=(n_seq, nt),
        in_specs=[pl.BlockSpec((tm, SHIFT_COLS), lambda b, c: (base + b * nt + c, 0)),
                  pl.BlockSpec((8, SHIFT_COLS),
                               lambda b, c: (jnp.maximum((base + b * nt + c) * (tm // 8) - 1, 0), 0)),
                  ] + _prep_weight_specs(2),
        out_specs=(ospec,) * 7,
        compiler_params=_cparams(("parallel", "parallel")),
        name="rwkv_prep_prompt",
    )(p_a, p_a, *wts)


def _rwkv_prep_sample(p_a, row0, n_seq, t_len, shift, wts):
    sb = _pick_tile(n_seq, 8, 64)
    tm = sb * t_len
    assert row0 % tm == 0 and t_len % 8 == 0
    base = row0 // tm
    out = jax.ShapeDtypeStruct((n_seq * t_len, C_A), F32)
    ospec = pl.BlockSpec((tm, C_A), lambda i: (i, 0))
    return pl.pallas_call(
        _prep_sample_kernel,
        out_shape=(out,) * 7,
        grid=(n_seq // sb,),
        in_specs=[pl.BlockSpec((tm, SHIFT_COLS), lambda i: (base + i, 0)),
                  pl.BlockSpec((sb, SHIFT_COLS), lambda i: (i, 0)),
                  ] + _prep_weight_specs(1),
        out_specs=(ospec,) * 7,
        compiler_params=_cparams(("parallel",)),
        name="rwkv_prep_sample",
    )(p_a, shift, *wts)


def _scan_kernel(has_s0, unroll, *refs):
    if has_s0:
        (r_ref, k_ref, v_ref, kk_ref, b_ref, lw_ref, g_ref, s0_ref,
         lnw_ref, lnb_ref, rk_ref, y_ref, s_ref) = refs
    else:
        (r_ref, k_ref, v_ref, kk_ref, b_ref, lw_ref, g_ref,
         lnw_ref, lnb_ref, rk_ref, y_ref, s_ref) = refs
        s0_ref = None
    n_seq, c_len, _ = r_ref.shape

    @pl.when(pl.program_id(1) == 0)
    def _():
        if has_s0:
            s_ref[...] = s0_ref[...]
        else:
            s_ref[...] = jnp.zeros_like(s_ref)

    ti = lax.broadcasted_iota(jnp.int32, (c_len, c_len), 0)
    si = lax.broadcasted_iota(jnp.int32, (c_len, c_len), 1)
    tri_incl = (si <= ti).astype(BF16)
    strict = (si < ti).astype(F32)
    incl = (si <= ti).astype(F32)
    incl_signed = jnp.concatenate([incl, -incl], axis=1)
    ones_h = _head_ones()
    lnw, lnb, rk = lnw_ref[...], lnb_ref[...], rk_ref[...]
    heads = range(N_HEADS)
    hsl = [slice(h * HEAD, (h + 1) * HEAD) for h in heads]

    def gsum(x):
        hi, lo = _split2(x)
        return jnp.dot(hi, ones_h, preferred_element_type=F32) + jnp.dot(lo, ones_h, preferred_element_type=F32)

    def load(s):
        return ([ref[s] for ref in (r_ref, k_ref, v_ref, kk_ref, b_ref, lw_ref, g_ref)],
                [s_ref[s, h] for h in heads])

    def compute(streams, states):
        r, k, v, kk, b, lw, g = streams
        cum = _dot_exact_lhs(tri_incl, lw)
        tot = cum[c_len - 1:c_len, :]
        g_inv = jnp.exp(-cum)
        g_end = jnp.exp(tot - cum)
        l_mat = jnp.concatenate([kk * jnp.exp(cum - lw), r * jnp.exp(cum)], axis=0).astype(BF16)
        kbh = jnp.concatenate([k * g_inv, b * g_inv], axis=0).astype(BF16)
        e_mat = jnp.concatenate([k * g_end, -(b * g_end)], axis=0).astype(BF16)
        g_tot = jnp.exp(tot)
        l_h = [l_mat[:, hs] for hs in hsl]
        v_h = [v[:, hs] for hs in hsl]
        n_ub = [_dot_nt(l_h[h][:c_len], kbh[c_len:, hsl[h]]) * strict for h in heads]
        a_vk = [_dot_nt(l_h[h][:c_len], kbh[:c_len, hsl[h]]) * strict for h in heads]
        pm = [_dot_nt(l_h[h], states[h]) for h in heads]
        x = [pm[h][:c_len] + _dot(a_vk[h], v_h[h]) for h in heads]
        x = [x[h] - _dot(n_ub[h], x[h]) for h in heads]
        pw = n_ub
        m = 2
        while m < c_len:
            pw = [_dot(p, p) for p in pw]
            x = [x[h] + _dot(pw[h], x[h]) for h in heads]
            m *= 2
        b_m = [_dot_nt(l_h[h][c_len:], kbh[:, hsl[h]]) * incl_signed for h in heads]
        vu = [jnp.concatenate([v_h[h], x[h]], axis=0) for h in heads]
        y = jnp.concatenate([pm[h][c_len:] + _dot(b_m[h], vu[h]) for h in heads], axis=1)
        new_states = [states[h] * g_tot[:, hsl[h]] + _dot_tn(vu[h], e_mat[:, hsl[h]]) for h in heads]
        mu = gsum(y) * (1.0 / HEAD)
        d = y - mu
        var = gsum(d * d) * (1.0 / HEAD)
        yn = d * lax.rsqrt(var + GN_EPS) * lnw + lnb
        bonus = gsum(r * k * rk) * v
        return (yn + bonus) * g, new_states

    def group(base):
        idx = [base + i for i in range(unroll)]
        loaded = [load(s) for s in idx]
        results = [compute(*d) for d in loaded]
        for s, (y, new_states) in zip(idx, results):
            y_ref[s] = y
            for h in heads:
                s_ref[s, h] = new_states[h]

    if n_seq == unroll:
        group(0)
    else:
        def body(i, carry):
            group(i * unroll)
            return carry
        lax.fori_loop(0, n_seq // unroll, body, 0)


def _wkv_scan(streams, s0, n_seq, t_len, c_len, sb, unroll, lnw, lnb, rk):
    nc = t_len // c_len
    streams = [a.reshape(n_seq, t_len, C_A) for a in streams]
    dspec = pl.BlockSpec((sb, c_len, C_A), lambda i, c: (i, c, 0))
    sspec = pl.BlockSpec((sb, N_HEADS, HEAD, HEAD), lambda i, c: (i, 0, 0, 0))
    wspec = pl.BlockSpec((1, C_A), lambda i, c: (0, 0))
    ins = list(streams) + ([s0] if s0 is not None else []) + [lnw, lnb, rk]
    in_specs = [dspec] * 7 + ([sspec] if s0 is not None else []) + [wspec] * 3
    y, s_new = pl.pallas_call(
        functools.partial(_scan_kernel, s0 is not None, unroll),
        out_shape=(jax.ShapeDtypeStruct((n_seq, t_len, C_A), F32),
                   jax.ShapeDtypeStruct((n_seq, N_HEADS, HEAD, HEAD), F32)),
        grid=(n_seq // sb, nc),
        in_specs=in_specs,
        out_specs=(dspec, sspec),
        compiler_params=_cparams(("parallel", "arbitrary")),
        name="wkv_scan_c%d" % c_len,
    )(*ins)
    return y.reshape(n_seq * t_len, C_A), s_new


def _conv_tail(z, dwb, lnw, lnb):
    z = z + dwb
    mu = jnp.mean(z, axis=-1, keepdims=True)
    d = z - mu
    var = jnp.mean(d * d, axis=-1, keepdims=True)
    z = d * lax.rsqrt(var + LN_EPS) * lnw + lnb
    return z * _sigmoid(z)


def _glu(pb, glu_b):
    u = pb + glu_b
    return u[:, :C_B] * _sigmoid(u[:, C_B:])


def _conv_prompt_kernel(n_pad, pb_ref, halo_ref, glub_ref, dw_ref, dwb_ref, lnw_ref, lnb_ref,
                        y_ref, st_ref, ext_ref):
    c = pl.program_id(1)
    tm = pb_ref.shape[0]
    glub = glub_ref[...]
    halo = jnp.where(c == 0, 0.0, _glu(halo_ref[...], glub))
    u = _glu(pb_ref[...], glub)
    t_glob = c * tm + lax.broadcasted_iota(jnp.int32, u.shape, 0)
    u = jnp.where(t_glob < n_pad, 0.0, u)
    ext_ref[0:CONV_HALO, :] = halo
    ext_ref[CONV_HALO:, :] = u
    off = CONV_HALO - (CONV_W - 1)
    rb = 32

    def blk(i, carry):
        base = pl.multiple_of(i * rb, rb)
        win = ext_ref[pl.ds(base, rb + CONV_HALO), :]
        acc = jnp.zeros((rb, C_B), F32)
        for s in range(8):
            taps = [j for j in range(CONV_W) if (off + j) % 8 == s]
            shifted = win if s == 0 else pltpu.roll(win, win.shape[0] - s, 0)
            for j in taps:
                a = off + j - s
                acc = acc + shifted[a:a + rb] * dw_ref[j:j + 1, :]
        y_ref[pl.ds(base, rb), :] = _conv_tail(acc, dwb_ref[...], lnw_ref[...], lnb_ref[...])
        return carry

    lax.fori_loop(0, tm // rb, blk, 0)

    @pl.when(c == pl.num_programs(1) - 1)
    def _():
        st_ref[0] = ext_ref[CONV_HALO + tm - (CONV_W - 1):CONV_HALO + tm, :]


def _conv_prompt(p_b, row0, n_seq, t_len, n_pad, wts):
    tm = _pick_tile(t_len, CONV_HALO, 704)
    nt = t_len // tm
    assert row0 % tm == 0
    base = row0 // tm
    z2 = lambda b, c: (0, 0)
    return pl.pallas_call(
        functools.partial(_conv_prompt_kernel, n_pad),
        out_shape=(jax.ShapeDtypeStruct((n_seq * t_len, C_B), F32),
                   jax.ShapeDtypeStruct((n_seq, CONV_W - 1, C_B), F32)),
        grid=(n_seq, nt),
        in_specs=[pl.BlockSpec((tm, 2 * C_B), lambda b, c: (base + b * nt + c, 0)),
                  pl.BlockSpec((CONV_HALO, 2 * C_B),
                               lambda b, c: (jnp.maximum((base + b * nt + c) * (tm // CONV_HALO) - 1, 0), 0)),
                  pl.BlockSpec((1, 2 * C_B), z2), pl.BlockSpec((CONV_W, C_B), z2),
                  pl.BlockSpec((1, C_B), z2), pl.BlockSpec((1, C_B), z2), pl.BlockSpec((1, C_B), z2)],
        out_specs=(pl.BlockSpec((tm, C_B), lambda b, c: (b * nt + c, 0)),
                   pl.BlockSpec((1, CONV_W - 1, C_B), lambda b, c: (b, 0, 0))),
        scratch_shapes=[pltpu.VMEM((CONV_HALO + tm, C_B), F32)],
        compiler_params=_cparams(("parallel", "arbitrary")),
        name="conv_prompt",
    )(p_b, p_b, *wts)


def _conv_sample_kernel(pb_ref, buf_ref, glub_ref, dw_ref, dwb_ref, lnw_ref, lnb_ref,
                        y_ref, st_ref, ext_ref):
    sb, t_len = buf_ref.shape[0], pb_ref.shape[0] // buf_ref.shape[0]
    hist = CONV_W - 1
    u = _glu(pb_ref[...], glub_ref[...])
    lead = ext_ref.shape[1] - hist - t_len
    ext_ref[:, lead:lead + hist, :] = buf_ref[...]
    ext_ref[:, lead + hist:, :] = u.reshape(sb, t_len, C_B)
    acc = jnp.zeros((sb, t_len, C_B), F32)
    for j in range(CONV_W):
        acc = acc + ext_ref[:, lead + j:lead + j + t_len, :] * dw_ref[j:j + 1, :]
    z = _conv_tail(acc.reshape(sb * t_len, C_B), dwb_ref[...], lnw_ref[...], lnb_ref[...])
    y_ref[...] = z
    st_ref[...] = ext_ref[:, lead + t_len:, :]


def _conv_sample(p_b, row0, n_seq, t_len, buf, wts):
    sb = _pick_tile(n_seq, 8, 32)
    tm = sb * t_len
    assert row0 % tm == 0 and t_len % 8 == 0
    base = row0 // tm
    hist = CONV_W - 1
    ext_rows = -(-(hist + t_len) // 8) * 8
    z1 = lambda i: (0, 0)
    return pl.pallas_call(
        _conv_sample_kernel,
        out_shape=(jax.ShapeDtypeStruct((n_seq * t_len, C_B), F32),
                   jax.ShapeDtypeStruct((n_seq, hist, C_B), F32)),
        grid=(n_seq // sb,),
        in_specs=[pl.BlockSpec((tm, 2 * C_B), lambda i: (base + i, 0)),
                  pl.BlockSpec((sb, hist, C_B), lambda i: (i, 0, 0)),
                  pl.BlockSpec((1, 2 * C_B), z1), pl.BlockSpec((CONV_W, C_B), z1),
                  pl.BlockSpec((1, C_B), z1), pl.BlockSpec((1, C_B), z1), pl.BlockSpec((1, C_B), z1)],
        out_specs=(pl.BlockSpec((tm, C_B), lambda i: (i, 0)),
                   pl.BlockSpec((sb, hist, C_B), lambda i: (i, 0, 0))),
        scratch_shapes=[pltpu.VMEM((sb, ext_rows, C_B), F32)],
        compiler_params=_cparams(("parallel",)),
        name="conv_sample",
    )(p_b, buf, *wts)


TILE_SUB = D_MODEL // LANES


def _store_token_tiles(ref, x, lead=()):
    rows = x.shape[0]
    for c in range(TILE_SUB):
        ref[lead + (pl.ds(c, rows, stride=TILE_SUB), slice(None))] = x[:, c * LANES:(c + 1) * LANES]


def _load_token_tiles(ref, rows, lead=()):
    return jnp.concatenate([ref[lead + (pl.ds(c, rows, stride=TILE_SUB), slice(None))]
                            for c in range(TILE_SUB)], axis=1)


def _out_route_kernel(n_first, yaf_ref, yas_ref, ybf_ref, ybs_ref, x_ref, woa_ref, wob_ref, g_ref,
                      wrh_ref, wrl_ref, br_ref, x1_ref, h2_ref, idx_ref, gate_ref, rank_ref, cnt_ref, run_ref):
    i = pl.program_id(0)
    first = i < n_first
    ya = jnp.where(first, yaf_ref[...], yas_ref[...])
    yb = jnp.where(first, ybf_ref[...], ybs_ref[...])
    x1 = x_ref[...] + _dot(ya, woa_ref[...]) + _dot(yb, wob_ref[...])
    x1_ref[...] = x1
    h2 = _rms(x1, g_ref[...])
    _store_token_tiles(h2_ref, h2)
    hi, lo = _split2(h2)
    d = functools.partial(jnp.dot, preferred_element_type=F32)
    logits = d(hi, wrh_ref[...]) + d(hi, wrl_ref[...]) + d(lo, wrh_ref[...]) + br_ref[...]
    lane = lax.broadcasted_iota(jnp.int32, logits.shape, 1)
    logits = jnp.where(lane < N_EXPERTS, logits, -jnp.inf)
    idx_out = jnp.zeros(logits.shape, jnp.int32)
    val_out = jnp.full(logits.shape, -jnp.inf, F32)
    onehots = []
    for kk in range(TOP_K):
        m = jnp.max(logits, axis=-1, keepdims=True)
        sel = jnp.min(jnp.where(logits == m, lane, LANES), axis=-1, keepdims=True)
        idx_out = jnp.where(lane == kk, sel, idx_out)
        val_out = jnp.where(lane == kk, m, val_out)
        hit = lane == sel
        onehots.append(hit.astype(F32))
        logits = jnp.where(hit, -jnp.inf, logits)
    e = jnp.exp(val_out - jnp.max(val_out, axis=-1, keepdims=True))
    idx_ref[...] = idx_out
    gate_ref[...] = e / jnp.sum(e, axis=-1, keepdims=True)

    @pl.when(i == 0)
    def _():
        run_ref[...] = jnp.zeros_like(run_ref)

    tm = logits.shape[0]
    total = onehots[0] + onehots[1] + onehots[2] + onehots[3]
    ti = lax.broadcasted_iota(jnp.int32, (tm, tm), 0)
    si = lax.broadcasted_iota(jnp.int32, (tm, tm), 1)
    before = jnp.dot((si < ti).astype(BF16), total.astype(BF16), preferred_element_type=F32) + run_ref[...]
    rank = jnp.zeros(logits.shape, F32)
    for kk in range(TOP_K):
        rank = jnp.where(lane == kk, jnp.sum(onehots[kk] * before, axis=-1, keepdims=True), rank)
    rank_ref[...] = rank.astype(jnp.int32)
    run_ref[...] += jnp.sum(total, axis=0, keepdims=True)
    cnt_ref[...] = run_ref[...]


def _out_route(ya_f, ya_s, yb_f, yb_s, x, wo_a, wo_b, g, wr_hi, wr_lo, b_r):
    n = x.shape[0]
    n_f = ya_f.shape[0]
    tm = ROW_TILE
    while n_f % tm or (n - n_f) % tm:
        tm //= 2
    assert tm % 8 == 0
    nf_t, ns_t = n_f // tm, (n - n_f) // tm
    z = lambda i: (0, 0)
    row = lambda w: pl.BlockSpec((tm, w), lambda i: (i, 0))
    fst = lambda w: pl.BlockSpec((tm, w), lambda i: (jnp.minimum(i, nf_t - 1), 0))
    snd = lambda w: pl.BlockSpec((tm, w), lambda i: (jnp.maximum(i - nf_t, 0), 0))
    return pl.pallas_call(
        functools.partial(_out_route_kernel, nf_t),
        out_shape=(jax.ShapeDtypeStruct((n, D_MODEL), F32), jax.ShapeDtypeStruct((n * TILE_SUB, LANES), F32),
                   jax.ShapeDtypeStruct((n, LANES), jnp.int32), jax.ShapeDtypeStruct((n, LANES), F32),
                   jax.ShapeDtypeStruct((n, LANES), jnp.int32), jax.ShapeDtypeStruct((1, LANES), F32)),
        grid=(nf_t + ns_t,),
        in_specs=[fst(C_A), snd(C_A), fst(C_B), snd(C_B), row(D_MODEL),
                  pl.BlockSpec((C_A, D_MODEL), z), pl.BlockSpec((C_B, D_MODEL), z),
                  pl.BlockSpec((1, D_MODEL), z),
                  pl.BlockSpec((D_MODEL, LANES), z), pl.BlockSpec((D_MODEL, LANES), z),
                  pl.BlockSpec((1, LANES), z)],
        out_specs=(row(D_MODEL), pl.BlockSpec((tm * TILE_SUB, LANES), lambda i: (i, 0)), row(LANES), row(LANES),
                   row(LANES), pl.BlockSpec((1, LANES), z)),
        scratch_shapes=[pltpu.VMEM((1, LANES), F32)],
        compiler_params=_cparams(("arbitrary",)),
        name="out_route",
    )(ya_f, ya_s, yb_f, yb_s, x, wo_a, wo_b, g, wr_hi, wr_lo, b_r)


FFN_BUFS = 3


def _ffn_kernel(n_tok, asg_ref, bexp_ref, nblk_ref, h2_hbm, wg_ref, bg_ref, wu_ref, bu_ref, wd_ref, bd_ref,
                out_hbm, *scratch):
    xbufs, ybufs = scratch[:FFN_BUFS], scratch[FFN_BUFS:2 * FFN_BUFS]
    wg_bf, wu_bf, wd_bf, gsem, ssem = scratch[2 * FFN_BUFS:]
    j = pl.program_id(0)
    last = pl.num_programs(0) - 1
    n_used = nblk_ref[0]
    rows = MOE_ROWS

    def gather(i, tab_row, s):
        tok = jnp.minimum(lax.shift_right_logical(asg_ref[tab_row, i], jnp.int32(2)), n_tok - 1)
        return pltpu.make_async_copy(h2_hbm.at[pl.ds(pl.multiple_of(tok * TILE_SUB, TILE_SUB), TILE_SUB), :],
                                     xbufs[s].at[pl.ds(i * TILE_SUB, TILE_SUB), :], gsem.at[s])

    def scatter(i, tab_row, s):
        dst = asg_ref[tab_row, i]
        return pltpu.make_async_copy(ybufs[s].at[pl.ds(i * TILE_SUB, TILE_SUB), :],
                                     out_hbm.at[pl.ds(pl.multiple_of(dst * TILE_SUB, TILE_SUB), TILE_SUB), :],
                                     ssem.at[s])

    def start_all(fn, unrolled):
        if unrolled:
            for i in range(rows):
                fn(i).start()
        else:
            def body(i, carry):
                fn(i).start()
                return carry
            lax.fori_loop(0, rows, body, 0)

    def wait_gather(s):
        pltpu.make_async_copy(h2_hbm.at[pl.ds(0, rows * TILE_SUB), :], xbufs[s], gsem.at[s]).wait()

    def wait_scatter(s):
        pltpu.make_async_copy(ybufs[s], out_hbm.at[pl.ds(0, rows * TILE_SUB), :], ssem.at[s]).wait()

    @pl.when(j == 0)
    def _():
        for yb in ybufs:
            yb[...] = jnp.zeros_like(yb)
        start_all(lambda i: gather(i, 1, 0), False)
        start_all(lambda i: gather(i, 2, 1), False)

    def step(slot):
        prev = (slot + FFN_BUFS - 1) % FFN_BUFS
        nxt = (slot + 1) % FFN_BUFS

        def start_next(unrolled):
            start_all(lambda i: gather(i, j + 3, prev), unrolled)
            start_all(lambda i: scatter(i, j, prev), unrolled)

        @pl.when(j >= 2)
        def _():
            wait_scatter(slot)

        wait_gather(slot)

        @pl.when(j < n_used)
        def _():
            new_expert = jnp.logical_or(j == 0, bexp_ref[j] != bexp_ref[jnp.maximum(j - 1, 0)])

            @pl.when(new_expert)
            def _():
                wg_bf[...] = wg_ref[0].astype(BF16)
                wu_bf[...] = wu_ref[0].astype(BF16)
                wd_bf[...] = wd_ref[0].astype(BF16)

            x = _load_token_tiles(xbufs[slot], rows).astype(BF16)
            start_next(True)
            gt = jnp.minimum(jnp.dot(x, wg_bf[...], preferred_element_type=F32) + bg_ref[0], SWIGLU_LIMIT)
            up = jnp.clip(jnp.dot(x, wu_bf[...], preferred_element_type=F32) + bu_ref[0],
                          -SWIGLU_LIMIT, SWIGLU_LIMIT)
            act = (up + 1.0) * gt * _sigmoid(SWIGLU_ALPHA * gt)
            y = jnp.dot(act.astype(BF16), wd_bf[...], preferred_element_type=F32) + bd_ref[0]
            _store_token_tiles(ybufs[slot], y)

        @pl.when(j >= n_used)
        def _():
            start_next(False)

        @pl.when(j == last)
        def _():
            for s in (nxt, prev):
                wait_gather(s)
                wait_scatter(s)

    for s in range(FFN_BUFS):
        pl.when(lax.rem(j, FFN_BUFS) == s)(functools.partial(step, s))


def _moe_ffn(asg, block_expert, n_used, h2t, w_gate, b_gate, w_up, b_up, w_down, b_down):
    n_steps = block_expert.shape[0]
    n_tok = h2t.shape[0] // TILE_SUB
    d_ff = w_gate.shape[2]
    out_rows = n_steps * MOE_ROWS
    wspec = lambda shape: pl.BlockSpec((1,) + shape, lambda j, a, be, nb: (be[j], 0, 0))
    tiles = (MOE_ROWS * TILE_SUB, LANES)
    return pl.pallas_call(
        functools.partial(_ffn_kernel, n_tok),
        out_shape=jax.ShapeDtypeStruct((out_rows * TILE_SUB, LANES), F32),
        grid_spec=pltpu.PrefetchScalarGridSpec(
            num_scalar_prefetch=3,
            grid=(n_steps,),
            in_specs=[pl.BlockSpec(memory_space=pl.ANY),
                      wspec((D_MODEL, d_ff)), wspec((1, d_ff)),
                      wspec((D_MODEL, d_ff)), wspec((1, d_ff)),
                      wspec((d_ff, D_MODEL)), wspec((1, D_MODEL))],
            out_specs=pl.BlockSpec(memory_space=pl.ANY),
            scratch_shapes=[pltpu.VMEM(tiles, F32)] * (2 * FFN_BUFS) + [
                pltpu.VMEM((D_MODEL, d_ff), BF16), pltpu.VMEM((D_MODEL, d_ff), BF16),
                pltpu.VMEM((d_ff, D_MODEL), BF16),
                pltpu.SemaphoreType.DMA((FFN_BUFS,)), pltpu.SemaphoreType.DMA((FFN_BUFS,))]),
        compiler_params=_cparams(("arbitrary",)),
        name="moe_ffn",
    )(asg, block_expert, n_used, h2t, w_gate, b_gate.reshape(N_EXPERTS, 1, d_ff),
      w_up, b_up.reshape(N_EXPERTS, 1, d_ff), w_down, b_down.reshape(N_EXPERTS, 1, D_MODEL))


def _combine_kernel(e_ref, x1_ref, gate_ref, g_ref, y_ref):
    tm = x1_ref.shape[0]
    gates = gate_ref[...]
    acc = x1_ref[...]
    for kk in range(TOP_K):
        yk = jnp.concatenate([e_ref[pl.ds(kk * TILE_SUB + c, tm, stride=TOP_K * TILE_SUB), :]
                              for c in range(TILE_SUB)], axis=1)
        acc = acc + yk * gates[:, kk:kk + 1]
    y_ref[...] = _rms(acc, g_ref[...])


def _combine(yt, x1, gates, g_final):
    n = x1.shape[0]
    tm = _pick_tile(n, 8, 256)
    return pl.pallas_call(
        _combine_kernel,
        out_shape=jax.ShapeDtypeStruct((n, D_MODEL), F32),
        grid=(n // tm,),
        in_specs=[pl.BlockSpec((tm * TOP_K * TILE_SUB, LANES), lambda i: (i, 0)),
                  pl.BlockSpec((tm, D_MODEL), lambda i: (i, 0)),
                  pl.BlockSpec((tm, LANES), lambda i: (i, 0)),
                  pl.BlockSpec((1, D_MODEL), lambda i: (0, 0))],
        out_specs=pl.BlockSpec((tm, D_MODEL), lambda i: (i, 0)),
        compiler_params=_cparams(("parallel",)),
        name="moe_combine",
    )(yt, x1, gates, g_final)


def _routing(idx, rank, counts):
    n = idx.shape[0]
    n_assign = n * TOP_K
    flat_e = idx[:, :TOP_K].reshape(-1)
    rank = rank[:, :TOP_K].reshape(-1)
    counts = counts[0, :N_EXPERTS].astype(jnp.int32)
    padded = (counts + MOE_ROWS - 1) // MOE_ROWS * MOE_ROWS
    pad_end = jnp.cumsum(padded)
    pad_start = pad_end - padded
    n_blocks = (n_assign + N_EXPERTS * (MOE_ROWS - 1) + MOE_ROWS - 1) // MOE_ROWS
    dest = (pad_start[flat_e] + rank).astype(jnp.int32) + MOE_ROWS
    tab_rows = n_blocks + 1 + FFN_BUFS
    q = jnp.arange(tab_rows * MOE_ROWS, dtype=jnp.int32) - MOE_ROWS
    e_of = jnp.sum((q[:, None] >= pad_end[None, :]).astype(jnp.int32), axis=1)
    waste = padded - counts
    waste_before = jnp.concatenate([jnp.zeros((1,), waste.dtype), jnp.cumsum(waste)])
    e_c = jnp.minimum(e_of, N_EXPERTS - 1)
    in_expert = MOE_ROWS + waste_before[e_c] + (q - pad_start[e_c] - counts[e_c])
    after = MOE_ROWS + waste_before[N_EXPERTS] + (q - pad_end[N_EXPERTS - 1])
    dump = jnp.where(q < 0, q + MOE_ROWS, jnp.where(e_of < N_EXPERTS, in_expert, after))
    asg = (n_assign + dump).astype(jnp.int32).at[dest].set(jnp.arange(n_assign, dtype=jnp.int32))
    n_used = (pad_end[-1] // MOE_ROWS).astype(jnp.int32)
    blk_start = jnp.arange(n_blocks + 1, dtype=jnp.int32) * MOE_ROWS
    block_expert = jnp.minimum(jnp.searchsorted(pad_end, blk_start, side='right'), N_EXPERTS - 1)
    last_used = block_expert[jnp.maximum(n_used - 1, 0)]
    block_expert = jnp.where(jnp.arange(n_blocks + 1) < n_used, block_expert, last_used).astype(jnp.int32)
    return asg.reshape(tab_rows, MOE_ROWS), block_expert, n_used.reshape(1)


def _forward(x_prompt, x_sample, state_wkv, state_shift, state_conv, meta_tokens, norm_mix, w_in,
             tshift_mu, decay_w0, decay_w2, iclr_a0, iclr_a2, gate_g2, k_k, k_a, r_k, lnx_w, lnx_b,
             glu_b, dw_weight, dw_bias, conv_ln_w, conv_ln_b, w_out, norm_ffn, w_router, b_router,
             w_gate, b_gate, w_up, b_up, w_down, b_down, norm_final):
    n_p, seq, _ = x_prompt.shape
    n_s, t_s, _ = x_sample.shape
    t_real = N_META + seq
    n_pad = (-t_real) % CHUNK
    t_p = t_real + n_pad
    rows_p = n_p * t_p
    rows_s = n_s * t_s
    lyr = 0

    meta = jnp.broadcast_to(meta_tokens.astype(F32)[None], (n_p, N_META, D_MODEL))
    xp = jnp.concatenate([jnp.zeros((n_p, n_pad, D_MODEL), F32), meta, x_prompt], axis=1)
    x_all = jnp.concatenate([xp.reshape(rows_p, D_MODEL), x_sample.reshape(rows_s, D_MODEL)], axis=0)

    row = lambda a: a[lyr].reshape(1, -1).astype(F32)
    w_in_bf = w_in[lyr].astype(BF16)
    p_a, p_b = _in_proj(x_all, row(norm_mix), w_in_bf[:, :SHIFT_COLS], w_in_bf[:, SHIFT_COLS:])

    w_wa = jnp.zeros((LORA_WA, 2 * C_A), F32)
    w_wa = w_wa.at[:D_DECAY_LORA, :C_A].set(decay_w2[lyr]).at[D_DECAY_LORA:, C_A:].set(iclr_a2[lyr])
    prep_w = (row(tshift_mu), row(decay_w0), row(iclr_a0), w_wa.astype(BF16), gate_g2[lyr].astype(BF16),
              row(k_k), row(k_a))
    st_p = _rwkv_prep_prompt(p_a, 0, n_p, t_p, prep_w)
    st_s = _rwkv_prep_sample(p_a, rows_p, n_s, t_s, state_shift[lyr], prep_w)

    scan_w = (row(lnx_w), row(lnx_b), row(r_k))
    ya_p, wkv_p = _wkv_scan(st_p, None, n_p, t_p, CHUNK, 2, 2, *scan_w)
    ya_s, wkv_s = _wkv_scan(st_s, state_wkv[lyr], n_s, t_s, t_s, _pick_tile(n_s, 2, 16), 2, *scan_w)

    conv_w = (row(glu_b), dw_weight[lyr], row(dw_bias), row(conv_ln_w), row(conv_ln_b))
    yb_p, conv_p = _conv_prompt(p_b, 0, n_p, t_p, n_pad, conv_w)
    yb_s, conv_s = _conv_sample(p_b, rows_p, n_s, t_s, state_conv[lyr], conv_w)

    w_out_bf = w_out[lyr].astype(BF16)
    wr = jnp.zeros((D_MODEL, LANES), F32).at[:, :N_EXPERTS].set(w_router[lyr])
    wr_hi = wr.astype(BF16)
    wr_lo = (wr - wr_hi.astype(F32)).astype(BF16)
    b_r = jnp.zeros((1, LANES), F32).at[0, :N_EXPERTS].set(b_router[lyr])
    x1, h2, idx, gates, rank, counts = _out_route(ya_p, ya_s, yb_p, yb_s, x_all, w_out_bf[:C_A], w_out_bf[C_A:],
                                                  row(norm_ffn), wr_hi, wr_lo, b_r)

    asg, block_expert, n_used = _routing(idx, rank, counts)
    yt = _moe_ffn(asg, block_expert, n_used, h2, w_gate[lyr], b_gate[lyr], w_up[lyr], b_up[lyr],
                  w_down[lyr], b_down[lyr])
    y = _combine(yt, x1, gates, norm_final.reshape(1, -1))

    y_prompt = y[:rows_p].reshape(n_p, t_p, D_MODEL)[:, n_pad + N_META:]
    y_sample = y[rows_p:].reshape(n_s, t_s, D_MODEL)
    shift_p = p_a[:rows_p].reshape(n_p, t_p, SHIFT_COLS)[:, -1]
    shift_s = p_a[rows_p:].reshape(n_s, t_s, SHIFT_COLS)[:, -1]
    return (y_prompt.astype(x_prompt.dtype), y_sample.astype(x_sample.dtype),
            wkv_p[None], shift_p[None], conv_p[None], wkv_s[None], shift_s[None], conv_s[None])


def kernel(x_prompt, x_sample, state_wkv, state_shift, state_conv, meta_tokens, norm_mix, w_in, tshift_mu, decay_w0, decay_w2, iclr_a0, iclr_a2, gate_g2, k_k, k_a, r_k, lnx_w, lnx_b, glu_b, dw_weight, dw_bias, conv_ln_w, conv_ln_b, w_out, norm_ffn, w_router, b_router, w_gate, b_gate, w_up, b_up, w_down, b_down, norm_final):
    assert w_in.shape[0] == 1, "single trunk layer"
    return _forward(x_prompt, x_sample, state_wkv, state_shift, state_conv, meta_tokens, norm_mix, w_in,
                    tshift_mu, decay_w0, decay_w2, iclr_a0, iclr_a2, gate_g2, k_k, k_a, r_k, lnx_w, lnx_b,
                    glu_b, dw_weight, dw_bias, conv_ln_w, conv_ln_b, w_out, norm_ffn, w_router, b_router,
                    w_gate, b_gate, w_up, b_up, w_down, b_down, norm_final)
```

```python
import functools

import jax
import jax.numpy as jnp
from jax import lax
from jax.experimental import pallas as pl
from jax.experimental.pallas import tpu as pltpu

F32 = jnp.float32
BF16 = jnp.bfloat16

D_MODEL = 1024
N_META = 16
C_A = 512
HEAD = 64
N_HEADS = C_A // HEAD
C_B = 512
CONV_W = 31
D_DECAY_LORA = 64
D_AAA_LORA = 64
D_GATE_LORA = 128
LORA_WA = D_DECAY_LORA + D_AAA_LORA
SHIFT_COLS = 3 * C_A + LORA_WA + D_GATE_LORA
N_EXPERTS = 32
TOP_K = 4
SWIGLU_LIMIT = 7.0
SWIGLU_ALPHA = 1.702
RMS_EPS = 1e-5
LN_EPS = 1e-5
GN_EPS = 64e-5

LANES = 128
CHUNK = 64
CONV_HALO = 32
ROW_TILE = 512
MOE_ROWS = 256
VMEM_LIMIT = 56 * 1024 * 1024


def _cparams(sem):
    return pltpu.CompilerParams(dimension_semantics=sem, vmem_limit_bytes=VMEM_LIMIT)


def _dot(a, b):
    return jnp.dot(a.astype(BF16), b.astype(BF16), preferred_element_type=F32)


def _dot_nt(a, b):
    return lax.dot_general(a.astype(BF16), b.astype(BF16), (((1,), (1,)), ((), ())),
                           preferred_element_type=F32)


def _dot_tn(a, b):
    return lax.dot_general(a.astype(BF16), b.astype(BF16), (((0,), (0,)), ((), ())),
                           preferred_element_type=F32)


def _split2(x):
    hi = x.astype(BF16)
    lo = (x - hi.astype(F32)).astype(BF16)
    return hi, lo


def _split3(x):
    hi = x.astype(BF16)
    r1 = x - hi.astype(F32)
    mid = r1.astype(BF16)
    lo = (r1 - mid.astype(F32)).astype(BF16)
    return hi, mid, lo


def _dot_exact_rhs(x, m_bf16):
    hi, mid, lo = _split3(x)
    d = functools.partial(jnp.dot, preferred_element_type=F32)
    return d(hi, m_bf16) + d(mid, m_bf16) + d(lo, m_bf16)


def _dot_exact_lhs(m_bf16, x):
    hi, mid, lo = _split3(x)
    d = functools.partial(jnp.dot, preferred_element_type=F32)
    return d(m_bf16, hi) + d(m_bf16, mid) + d(m_bf16, lo)


def _rms(x, g):
    return x * lax.rsqrt(jnp.mean(x * x, axis=-1, keepdims=True) + RMS_EPS) * g


def _sigmoid(x):
    return 1.0 / (1.0 + jnp.exp(-x))


def _head_ones():
    i = lax.broadcasted_iota(jnp.int32, (C_A, C_A), 0) // HEAD
    j = lax.broadcasted_iota(jnp.int32, (C_A, C_A), 1) // HEAD
    return (i == j).astype(BF16)


def _pick_tile(n, mult, cap):
    best = mult
    t = mult
    while t <= min(n, cap):
        if n % t == 0:
            best = t
        t += mult
    assert n % best == 0, (n, mult)
    return best


def _in_proj_kernel(x_ref, g_ref, wa_ref, wb_ref, pa_ref, pb_ref):
    h = _rms(x_ref[...], g_ref[...]).astype(BF16)
    pa_ref[...] = jnp.dot(h, wa_ref[...], preferred_element_type=F32)
    pb_ref[...] = jnp.dot(h, wb_ref[...], preferred_element_type=F32)


def _in_proj(x, g, w_a, w_b):
    n = x.shape[0]
    tm = _pick_tile(n, 8, ROW_TILE)
    return pl.pallas_call(
        _in_proj_kernel,
        out_shape=(jax.ShapeDtypeStruct((n, SHIFT_COLS), F32), jax.ShapeDtypeStruct((n, 2 * C_B), F32)),
        grid=(n // tm,),
        in_specs=[pl.BlockSpec((tm, D_MODEL), lambda i: (i, 0)),
                  pl.BlockSpec((1, D_MODEL), lambda i: (0, 0)),
                  pl.BlockSpec((D_MODEL, SHIFT_COLS), lambda i: (0, 0)),
                  pl.BlockSpec((D_MODEL, 2 * C_B), lambda i: (0, 0))],
        out_specs=(pl.BlockSpec((tm, SHIFT_COLS), lambda i: (i, 0)),
                   pl.BlockSpec((tm, 2 * C_B), lambda i: (i, 0))),
        compiler_params=_cparams(("parallel",)),
        name="in_proj",
    )(x, g, w_a, w_b)


def _prep_math(p, pprev, mu, w0, a0, w_wa, w_g, k_k, k_a, ones_h, outs):
    r_ref, k_ref, v_ref, kk_ref, b_ref, lw_ref, g_ref = outs
    xs = p + (pprev - p) * mu
    r = xs[:, 0:C_A]
    k = xs[:, C_A:2 * C_A]
    v = xs[:, 2 * C_A:3 * C_A]
    lo = xs[:, 3 * C_A:3 * C_A + LORA_WA]
    g_lo = xs[:, 3 * C_A + LORA_WA:]
    lane = lax.broadcasted_iota(jnp.int32, lo.shape, 1)
    lo = jnp.where(lane < D_DECAY_LORA, jnp.tanh(lo), lo)
    wa = _dot(lo, w_wa)
    z = -(w0 + wa[:, :C_A])
    w = -(jnp.maximum(z, 0.0) + jnp.log(1.0 + jnp.exp(-jnp.abs(z)))) - 0.5
    lw_ref[...] = -jnp.exp(w)
    a = _sigmoid(a0 + wa[:, C_A:])
    g_ref[...] = _dot(_sigmoid(g_lo), w_g)
    kk = k * k_k
    hi, lo2 = _split2(kk * kk)
    ss = jnp.dot(hi, ones_h, preferred_element_type=F32) + jnp.dot(lo2, ones_h, preferred_element_type=F32)
    kk = kk / jnp.maximum(jnp.sqrt(ss), 1e-12)
    r_ref[...] = r
    v_ref[...] = v
    k_ref[...] = k * (1.0 + (a - 1.0) * k_a)
    kk_ref[...] = kk
    b_ref[...] = kk * a


def _prep_prompt_kernel(p_ref, halo_ref, mu_ref, w0_ref, a0_ref, wwa_ref, wg_ref, kk_ref_w, ka_ref, *outs):
    p = p_ref[...]
    prev_last = jnp.where(pl.program_id(1) == 0, 0.0, halo_ref[7:8, :])
    rolled = pltpu.roll(p, 1, 0)
    row = lax.broadcasted_iota(jnp.int32, p.shape, 0)
    pprev = jnp.where(row == 0, prev_last, rolled)
    _prep_math(p, pprev, mu_ref[...], w0_ref[...], a0_ref[...], wwa_ref[...], wg_ref[...],
               kk_ref_w[...], ka_ref[...], _head_ones(), outs)


def _prep_sample_kernel(p_ref, shift_ref, mu_ref, w0_ref, a0_ref, wwa_ref, wg_ref, kk_ref_w, ka_ref,
                        *outs_and_scratch):
    outs = outs_and_scratch
    p = p_ref[...]
    sb = shift_ref.shape[0]
    t_len = p.shape[0] // sb
    first = jnp.broadcast_to(shift_ref[...][:, None, :], (sb, t_len, SHIFT_COLS)).reshape(p.shape)
    row = lax.broadcasted_iota(jnp.int32, p.shape, 0)
    pprev = jnp.where(row % t_len == 0, first, pltpu.roll(p, 1, 0))
    _prep_math(p, pprev, mu_ref[...], w0_ref[...], a0_ref[...], wwa_ref[...], wg_ref[...],
               kk_ref_w[...], ka_ref[...], _head_ones(), outs)


def _prep_weight_specs(nd):
    z = (lambda *_: (0, 0))
    del nd
    return [pl.BlockSpec((1, SHIFT_COLS), z), pl.BlockSpec((1, C_A), z), pl.BlockSpec((1, C_A), z),
            pl.BlockSpec((LORA_WA, 2 * C_A), z), pl.BlockSpec((D_GATE_LORA, C_A), z),
            pl.BlockSpec((1, C_A), z), pl.BlockSpec((1, C_A), z)]


def _rwkv_prep_prompt(p_a, row0, n_seq, t_len, wts):
    tm = _pick_tile(t_len, 8, 704)
    nt = t_len // tm
    assert row0 % tm == 0
    base = row0 // tm
    out = jax.ShapeDtypeStruct((n_seq * t_len, C_A), F32)
    ospec = pl.BlockSpec((tm, C_A), lambda b, c: (b * nt + c, 0))
    return pl.pallas_call(
        _prep_prompt_kernel,
        out_shape=(out,) * 7,
        grid=(n_seq, nt),
        in_specs=[pl.BlockSpec((tm, SHIFT_COLS), lambda b, c: (base + b * nt + c, 0)),
                  pl.BlockSpec((8, SHIFT_COLS),
                               lambda b, c: (jnp.maximum((base + b * nt + c) * (tm // 8) - 1, 0), 0)),
                  ] + _prep_weight_specs(2),
        out_specs=(ospec,) * 7,
        compiler_params=_cparams(("parallel", "parallel")),
        name="rwkv_prep_prompt",
    )(p_a, p_a, *wts)


def _rwkv_prep_sample(p_a, row0, n_seq, t_len, shift, wts):
    sb = _pick_tile(n_seq, 8, 64)
    tm = sb * t_len
    assert row0 % tm == 0 and t_len % 8 == 0
    base = row0 // tm
    out = jax.ShapeDtypeStruct((n_seq * t_len, C_A), F32)
    ospec = pl.BlockSpec((tm, C_A), lambda i: (i, 0))
    return pl.pallas_call(
        _prep_sample_kernel,
        out_shape=(out,) * 7,
        grid=(n_seq // sb,),
        in_specs=[pl.BlockSpec((tm, SHIFT_COLS), lambda i: (base + i, 0)),
                  pl.BlockSpec((sb, SHIFT_COLS), lambda i: (i, 0)),
                  ] + _prep_weight_specs(1),
        out_specs=(ospec,) * 7,
        compiler_params=_cparams(("parallel",)),
        name="rwkv_prep_sample",
    )(p_a, shift, *wts)


def _scan_kernel(has_s0, unroll, *refs):
    if has_s0:
        (r_ref, k_ref, v_ref, kk_ref, b_ref, lw_ref, g_ref, s0_ref,
         lnw_ref, lnb_ref, rk_ref, y_ref, s_ref) = refs
    else:
        (r_ref, k_ref, v_ref, kk_ref, b_ref, lw_ref, g_ref,
         lnw_ref, lnb_ref, rk_ref, y_ref, s_ref) = refs
        s0_ref = None
    n_seq, c_len, _ = r_ref.shape

    @pl.when(pl.program_id(1) == 0)
    def _():
        if has_s0:
            s_ref[...] = s0_ref[...]
        else:
            s_ref[...] = jnp.zeros_like(s_ref)

    ti = lax.broadcasted_iota(jnp.int32, (c_len, c_len), 0)
    si = lax.broadcasted_iota(jnp.int32, (c_len, c_len), 1)
    tri_incl = (si <= ti).astype(BF16)
    strict = (si < ti).astype(F32)
    incl = (si <= ti).astype(F32)
    incl_signed = jnp.concatenate([incl, -incl], axis=1)
    ones_h = _head_ones()
    lnw, lnb, rk = lnw_ref[...], lnb_ref[...], rk_ref[...]
    heads = range(N_HEADS)
    hsl = [slice(h * HEAD, (h + 1) * HEAD) for h in heads]

    def gsum(x):
        hi, lo = _split2(x)
        return jnp.dot(hi, ones_h, preferred_element_type=F32) + jnp.dot(lo, ones_h, preferred_element_type=F32)

    def load(s):
        return ([ref[s] for ref in (r_ref, k_ref, v_ref, kk_ref, b_ref, lw_ref, g_ref)],
                [s_ref[s, h] for h in heads])

    def compute(streams, states):
        r, k, v, kk, b, lw, g = streams
        cum = _dot_exact_lhs(tri_incl, lw)
        tot = cum[c_len - 1:c_len, :]
        g_inv = jnp.exp(-cum)
        g_end = jnp.exp(tot - cum)
        l_mat = jnp.concatenate([kk * jnp.exp(cum - lw), r * jnp.exp(cum)], axis=0).astype(BF16)
        kbh = jnp.concatenate([k * g_inv, b * g_inv], axis=0).astype(BF16)
        e_mat = jnp.concatenate([k * g_end, -(b * g_end)], axis=0).astype(BF16)
        g_tot = jnp.exp(tot)
        l_h = [l_mat[:, hs] for hs in hsl]
        v_h = [v[:, hs] for hs in hsl]
        n_ub = [_dot_nt(l_h[h][:c_len], kbh[c_len:, hsl[h]]) * strict for h in heads]
        a_vk = [_dot_nt(l_h[h][:c_len], kbh[:c_len, hsl[h]]) * strict for h in heads]
        pm = [_dot_nt(l_h[h], states[h]) for h in heads]
        x = [pm[h][:c_len] + _dot(a_vk[h], v_h[h]) for h in heads]
        x = [x[h] - _dot(n_ub[h], x[h]) for h in heads]
        pw = n_ub
        m = 2
        while m < c_len:
            pw = [_dot(p, p) for p in pw]
            x = [x[h] + _dot(pw[h], x[h]) for h in heads]
            m *= 2
        b_m = [_dot_nt(l_h[h][c_len:], kbh[:, hsl[h]]) * incl_signed for h in heads]
        vu = [jnp.concatenate([v_h[h], x[h]], axis=0) for h in heads]
        y = jnp.concatenate([pm[h][c_len:] + _dot(b_m[h], vu[h]) for h in heads], axis=1)
        new_states = [states[h] * g_tot[:, hsl[h]] + _dot_tn(vu[h], e_mat[:, hsl[h]]) for h in heads]
        mu = gsum(y) * (1.0 / HEAD)
        d = y - mu
        var = gsum(d * d) * (1.0 / HEAD)
        yn = d * lax.rsqrt(var + GN_EPS) * lnw + lnb
        bonus = gsum(r * k * rk) * v
        return (yn + bonus) * g, new_states

    def group(base):
        idx = [base + i for i in range(unroll)]
        loaded = [load(s) for s in idx]
        results = [compute(*d) for d in loaded]
        for s, (y, new_states) in zip(idx, results):
            y_ref[s] = y
            for h in heads:
                s_ref[s, h] = new_states[h]

    if n_seq == unroll:
        group(0)
    else:
        def body(i, carry):
            group(i * unroll)
            return carry
        lax.fori_loop(0, n_seq // unroll, body, 0)


def _wkv_scan(streams, s0, n_seq, t_len, c_len, sb, unroll, lnw, lnb, rk):
    nc = t_len // c_len
    streams = [a.reshape(n_seq, t_len, C_A) for a in streams]
    dspec = pl.BlockSpec((sb, c_len, C_A), lambda i, c: (i, c, 0))
    sspec = pl.BlockSpec((sb, N_HEADS, HEAD, HEAD), lambda i, c: (i, 0, 0, 0))
    wspec = pl.BlockSpec((1, C_A), lambda i, c: (0, 0))
    ins = list(streams) + ([s0] if s0 is not None else []) + [lnw, lnb, rk]
    in_specs = [dspec] * 7 + ([sspec] if s0 is not None else []) + [wspec] * 3
    y, s_new = pl.pallas_call(
        functools.partial(_scan_kernel, s0 is not None, unroll),
        out_shape=(jax.ShapeDtypeStruct((n_seq, t_len, C_A), F32),
                   jax.ShapeDtypeStruct((n_seq, N_HEADS, HEAD, HEAD), F32)),
        grid=(n_seq // sb, nc),
        in_specs=in_specs,
        out_specs=(dspec, sspec),
        compiler_params=_cparams(("parallel", "arbitrary")),
        name="wkv_scan_c%d" % c_len,
    )(*ins)
    return y.reshape(n_seq * t_len, C_A), s_new


def _conv_tail(z, dwb, lnw, lnb):
    z = z + dwb
    mu = jnp.mean(z, axis=-1, keepdims=True)
    d = z - mu
    var = jnp.mean(d * d, axis=-1, keepdims=True)
    z = d * lax.rsqrt(var + LN_EPS) * lnw + lnb
    return z * _sigmoid(z)


def _glu(pb, glu_b):
    u = pb + glu_b
    return u[:, :C_B] * _sigmoid(u[:, C_B:])


def _conv_prompt_kernel(n_pad, pb_ref, halo_ref, glub_ref, dw_ref, dwb_ref, lnw_ref, lnb_ref,
                        y_ref, st_ref, ext_ref):
    c = pl.program_id(1)
    tm = pb_ref.shape[0]
    glub = glub_ref[...]
    halo = jnp.where(c == 0, 0.0, _glu(halo_ref[...], glub))
    u = _glu(pb_ref[...], glub)
    t_glob = c * tm + lax.broadcasted_iota(jnp.int32, u.shape, 0)
    u = jnp.where(t_glob < n_pad, 0.0, u)
    ext_ref[0:CONV_HALO, :] = halo
    ext_ref[CONV_HALO:, :] = u
    off = CONV_HALO - (CONV_W - 1)
    rb = 32

    def blk(i, carry):
        base = pl.multiple_of(i * rb, rb)
        win = ext_ref[pl.ds(base, rb + CONV_HALO), :]
        acc = jnp.zeros((rb, C_B), F32)
        for s in range(8):
            taps = [j for j in range(CONV_W) if (off + j) % 8 == s]
            shifted = win if s == 0 else pltpu.roll(win, win.shape[0] - s, 0)
            for j in taps:
                a = off + j - s
                acc = acc + shifted[a:a + rb] * dw_ref[j:j + 1, :]
        y_ref[pl.ds(base, rb), :] = _conv_tail(acc, dwb_ref[...], lnw_ref[...], lnb_ref[...])
        return carry

    lax.fori_loop(0, tm // rb, blk, 0)

    @pl.when(c == pl.num_programs(1) - 1)
    def _():
        st_ref[0] = ext_ref[CONV_HALO + tm - (CONV_W - 1):CONV_HALO + tm, :]


def _conv_prompt(p_b, row0, n_seq, t_len, n_pad, wts):
    tm = _pick_tile(t_len, CONV_HALO, 704)
    nt = t_len // tm
    assert row0 % tm == 0
    base = row0 // tm
    z2 = lambda b, c: (0, 0)
    return pl.pallas_call(
        functools.partial(_conv_prompt_kernel, n_pad),
        out_shape=(jax.ShapeDtypeStruct((n_seq * t_len, C_B), F32),
                   jax.ShapeDtypeStruct((n_seq, CONV_W - 1, C_B), F32)),
        grid=(n_seq, nt),
        in_specs=[pl.BlockSpec((tm, 2 * C_B), lambda b, c: (base + b * nt + c, 0)),
                  pl.BlockSpec((CONV_HALO, 2 * C_B),
                               lambda b, c: (jnp.maximum((base + b * nt + c) * (tm // CONV_HALO) - 1, 0), 0)),
                  pl.BlockSpec((1, 2 * C_B), z2), pl.BlockSpec((CONV_W, C_B), z2),
                  pl.BlockSpec((1, C_B), z2), pl.BlockSpec((1, C_B), z2), pl.BlockSpec((1, C_B), z2)],
        out_specs=(pl.BlockSpec((tm, C_B), lambda b, c: (b * nt + c, 0)),
                   pl.BlockSpec((1, CONV_W - 1, C_B), lambda b, c: (b, 0, 0))),
        scratch_shapes=[pltpu.VMEM((CONV_HALO + tm, C_B), F32)],
        compiler_params=_cparams(("parallel", "arbitrary")),
        name="conv_prompt",
    )(p_b, p_b, *wts)


def _conv_sample_kernel(pb_ref, buf_ref, glub_ref, dw_ref, dwb_ref, lnw_ref, lnb_ref,
                        y_ref, st_ref, ext_ref):
    sb, t_len = buf_ref.shape[0], pb_ref.shape[0] // buf_ref.shape[0]
    hist = CONV_W - 1
    u = _glu(pb_ref[...], glub_ref[...])
    lead = ext_ref.shape[1] - hist - t_len
    ext_ref[:, lead:lead + hist, :] = buf_ref[...]
    ext_ref[:, lead + hist:, :] = u.reshape(sb, t_len, C_B)
    acc = jnp.zeros((sb, t_len, C_B), F32)
    for j in range(CONV_W):
        acc = acc + ext_ref[:, lead + j:lead + j + t_len, :] * dw_ref[j:j + 1, :]
    z = _conv_tail(acc.reshape(sb * t_len, C_B), dwb_ref[...], lnw_ref[...], lnb_ref[...])
    y_ref[...] = z
    st_ref[...] = ext_ref[:, lead + t_len:, :]


def _conv_sample(p_b, row0, n_seq, t_len, buf, wts):
    sb = _pick_tile(n_seq, 8, 32)
    tm = sb * t_len
    assert row0 % tm == 0 and t_len % 8 == 0
    base = row0 // tm
    hist = CONV_W - 1
    ext_rows = -(-(hist + t_len) // 8) * 8
    z1 = lambda i: (0, 0)
    return pl.pallas_call(
        _conv_sample_kernel,
        out_shape=(jax.ShapeDtypeStruct((n_seq * t_len, C_B), F32),
                   jax.ShapeDtypeStruct((n_seq, hist, C_B), F32)),
        grid=(n_seq // sb,),
        in_specs=[pl.BlockSpec((tm, 2 * C_B), lambda i: (base + i, 0)),
                  pl.BlockSpec((sb, hist, C_B), lambda i: (i, 0, 0)),
                  pl.BlockSpec((1, 2 * C_B), z1), pl.BlockSpec((CONV_W, C_B), z1),
                  pl.BlockSpec((1, C_B), z1), pl.BlockSpec((1, C_B), z1), pl.BlockSpec((1, C_B), z1)],
        out_specs=(pl.BlockSpec((tm, C_B), lambda i: (i, 0)),
                   pl.BlockSpec((sb, hist, C_B), lambda i: (i, 0, 0))),
        scratch_shapes=[pltpu.VMEM((sb, ext_rows, C_B), F32)],
        compiler_params=_cparams(("parallel",)),
        name="conv_sample",
    )(p_b, buf, *wts)


TILE_SUB = D_MODEL // LANES


def _store_token_tiles(ref, x, lead=()):
    rows = x.shape[0]
    for c in range(TILE_SUB):
        ref[lead + (pl.ds(c, rows, stride=TILE_SUB), slice(None))] = x[:, c * LANES:(c + 1) * LANES]


def _load_token_tiles(ref, rows, lead=()):
    return jnp.concatenate([ref[lead + (pl.ds(c, rows, stride=TILE_SUB), slice(None))]
                            for c in range(TILE_SUB)], axis=1)


def _out_route_kernel(n_first, yaf_ref, yas_ref, ybf_ref, ybs_ref, x_ref, woa_ref, wob_ref, g_ref,
                      wrh_ref, wrl_ref, br_ref, x1_ref, h2_ref, idx_ref, gate_ref, rank_ref, cnt_ref, run_ref):
    i = pl.program_id(0)
    first = i < n_first
    ya = jnp.where(first, yaf_ref[...], yas_ref[...])
    yb = jnp.where(first, ybf_ref[...], ybs_ref[...])
    x1 = x_ref[...] + _dot(ya, woa_ref[...]) + _dot(yb, wob_ref[...])
    x1_ref[...] = x1
    h2 = _rms(x1, g_ref[...])
    _store_token_tiles(h2_ref, h2)
    hi, lo = _split2(h2)
    d = functools.partial(jnp.dot, preferred_element_type=F32)
    logits = d(hi, wrh_ref[...]) + d(hi, wrl_ref[...]) + d(lo, wrh_ref[...]) + br_ref[...]
    lane = lax.broadcasted_iota(jnp.int32, logits.shape, 1)
    logits = jnp.where(lane < N_EXPERTS, logits, -jnp.inf)
    idx_out = jnp.zeros(logits.shape, jnp.int32)
    val_out = jnp.full(logits.shape, -jnp.inf, F32)
    onehots = []
    for kk in range(TOP_K):
        m = jnp.max(logits, axis=-1, keepdims=True)
        sel = jnp.min(jnp.where(logits == m, lane, LANES), axis=-1, keepdims=True)
        idx_out = jnp.where(lane == kk, sel, idx_out)
        val_out = jnp.where(lane == kk, m, val_out)
        hit = lane == sel
        onehots.append(hit.astype(F32))
        logits = jnp.where(hit, -jnp.inf, logits)
    e = jnp.exp(val_out - jnp.max(val_out, axis=-1, keepdims=True))
    idx_ref[...] = idx_out
    gate_ref[...] = e / jnp.sum(e, axis=-1, keepdims=True)

    @pl.when(i == 0)
    def _():
        run_ref[...] = jnp.zeros_like(run_ref)

    tm = logits.shape[0]
    total = onehots[0] + onehots[1] + onehots[2] + onehots[3]
    ti = lax.broadcasted_iota(jnp.int32, (tm, tm), 0)
    si = lax.broadcasted_iota(jnp.int32, (tm, tm), 1)
    before = jnp.dot((si < ti).astype(BF16), total.astype(BF16), preferred_element_type=F32) + run_ref[...]
    rank = jnp.zeros(logits.shape, F32)
    for kk in range(TOP_K):
        rank = jnp.where(lane == kk, jnp.sum(onehots[kk] * before, axis=-1, keepdims=True), rank)
    rank_ref[...] = rank.astype(jnp.int32)
    run_ref[...] += jnp.sum(total, axis=0, keepdims=True)
    cnt_ref[...] = run_ref[...]


def _out_route(ya_f, ya_s, yb_f, yb_s, x, wo_a, wo_b, g, wr_hi, wr_lo, b_r):
    n = x.shape[0]
    n_f = ya_f.shape[0]
    tm = ROW_TILE
    while n_f % tm or (n - n_f) % tm:
        tm //= 2
    assert tm % 8 == 0
    nf_t, ns_t = n_f // tm, (n - n_f) // tm
    z = lambda i: (0, 0)
    row = lambda w: pl.BlockSpec((tm, w), lambda i: (i, 0))
    fst = lambda w: pl.BlockSpec((tm, w), lambda i: (jnp.minimum(i, nf_t - 1), 0))
    snd = lambda w: pl.BlockSpec((tm, w), lambda i: (jnp.maximum(i - nf_t, 0), 0))
    return pl.pallas_call(
        functools.partial(_out_route_kernel, nf_t),
        out_shape=(jax.ShapeDtypeStruct((n, D_MODEL), F32), jax.ShapeDtypeStruct((n * TILE_SUB, LANES), F32),
                   jax.ShapeDtypeStruct((n, LANES), jnp.int32), jax.ShapeDtypeStruct((n, LANES), F32),
                   jax.ShapeDtypeStruct((n, LANES), jnp.int32), jax.ShapeDtypeStruct((1, LANES), F32)),
        grid=(nf_t + ns_t,),
        in_specs=[fst(C_A), snd(C_A), fst(C_B), snd(C_B), row(D_MODEL),
                  pl.BlockSpec((C_A, D_MODEL), z), pl.BlockSpec((C_B, D_MODEL), z),
                  pl.BlockSpec((1, D_MODEL), z),
                  pl.BlockSpec((D_MODEL, LANES), z), pl.BlockSpec((D_MODEL, LANES), z),
                  pl.BlockSpec((1, LANES), z)],
        out_specs=(row(D_MODEL), pl.BlockSpec((tm * TILE_SUB, LANES), lambda i: (i, 0)), row(LANES), row(LANES),
                   row(LANES), pl.BlockSpec((1, LANES), z)),
        scratch_shapes=[pltpu.VMEM((1, LANES), F32)],
        compiler_params=_cparams(("arbitrary",)),
        name="out_route",
    )(ya_f, ya_s, yb_f, yb_s, x, wo_a, wo_b, g, wr_hi, wr_lo, b_r)


FFN_BUFS = 3


def _ffn_kernel(n_tok, asg_ref, bexp_ref, nblk_ref, h2_hbm, wg_ref, bg_ref, wu_ref, bu_ref, wd_ref, bd_ref,
                out_hbm, *scratch):
    xbufs, ybufs = scratch[:FFN_BUFS], scratch[FFN_BUFS:2 * FFN_BUFS]
    wg_bf, wu_bf, wd_bf, gsem, ssem = scratch[2 * FFN_BUFS:]
    j = pl.program_id(0)
    last = pl.num_programs(0) - 1
    n_used = nblk_ref[0]
    rows = MOE_ROWS

    def gather(i, tab_row, s):
        tok = jnp.minimum(lax.shift_right_logical(asg_ref[tab_row, i], jnp.int32(2)), n_tok - 1)
        return pltpu.make_async_copy(h2_hbm.at[pl.ds(pl.multiple_of(tok * TILE_SUB, TILE_SUB), TILE_SUB), :],
                                     xbufs[s].at[pl.ds(i * TILE_SUB, TILE_SUB), :], gsem.at[s])

    def scatter(i, tab_row, s):
        dst = asg_ref[tab_row, i]
        return pltpu.make_async_copy(ybufs[s].at[pl.ds(i * TILE_SUB, TILE_SUB), :],
                                     out_hbm.at[pl.ds(pl.multiple_of(dst * TILE_SUB, TILE_SUB), TILE_SUB), :],
                                     ssem.at[s])

    def start_all(fn, unrolled):
        if unrolled:
            for i in range(rows):
                fn(i).start(priority=i % 2)
        else:
            def body(i, carry):
                fn(i).start()
                return carry
            lax.fori_loop(0, rows, body, 0)

    def wait_gather(s):
        pltpu.make_async_copy(h2_hbm.at[pl.ds(0, rows * TILE_SUB), :], xbufs[s], gsem.at[s]).wait()

    def wait_scatter(s):
        pltpu.make_async_copy(ybufs[s], out_hbm.at[pl.ds(0, rows * TILE_SUB), :], ssem.at[s]).wait()

    @pl.when(j == 0)
    def _():
        for yb in ybufs:
            yb[...] = jnp.zeros_like(yb)
        start_all(lambda i: gather(i, 1, 0), False)
        start_all(lambda i: gather(i, 2, 1), False)

    def step(slot):
        prev = (slot + FFN_BUFS - 1) % FFN_BUFS
        nxt = (slot + 1) % FFN_BUFS

        def start_next(unrolled):
            start_all(lambda i: gather(i, j + 3, prev), unrolled)
            start_all(lambda i: scatter(i, j, prev), unrolled)

        @pl.when(j >= 2)
        def _():
            wait_scatter(slot)

        wait_gather(slot)

        @pl.when(j < n_used)
        def _():
            new_expert = jnp.logical_or(j == 0, bexp_ref[j] != bexp_ref[jnp.maximum(j - 1, 0)])

            @pl.when(new_expert)
            def _():
                wg_bf[...] = wg_ref[0].astype(BF16)
                wu_bf[...] = wu_ref[0].astype(BF16)
                wd_bf[...] = wd_ref[0].astype(BF16)

            x = _load_token_tiles(xbufs[slot], rows).astype(BF16)
            start_next(True)
            gt = jnp.minimum(jnp.dot(x, wg_bf[...], preferred_element_type=F32) + bg_ref[0], SWIGLU_LIMIT)
            up = jnp.clip(jnp.dot(x, wu_bf[...], preferred_element_type=F32) + bu_ref[0],
                          -SWIGLU_LIMIT, SWIGLU_LIMIT)
            act = (up + 1.0) * gt * _sigmoid(SWIGLU_ALPHA * gt)
            y = jnp.dot(act.astype(BF16), wd_bf[...], preferred_element_type=F32) + bd_ref[0]
            _store_token_tiles(ybufs[slot], y)

        @pl.when(j >= n_used)
        def _():
            start_next(False)

        @pl.when(j == last)
        def _():
            for s in (nxt, prev):
                wait_gather(s)
                wait_scatter(s)

    for s in range(FFN_BUFS):
        pl.when(lax.rem(j, FFN_BUFS) == s)(functools.partial(step, s))


def _moe_ffn(asg, block_expert, n_used, h2t, w_gate, b_gate, w_up, b_up, w_down, b_down):
    n_steps = block_expert.shape[0]
    n_tok = h2t.shape[0] // TILE_SUB
    d_ff = w_gate.shape[2]
    out_rows = n_steps * MOE_ROWS
    wspec = lambda shape: pl.BlockSpec((1,) + shape, lambda j, a, be, nb: (be[j], 0, 0))
    tiles = (MOE_ROWS * TILE_SUB, LANES)
    return pl.pallas_call(
        functools.partial(_ffn_kernel, n_tok),
        out_shape=jax.ShapeDtypeStruct((out_rows * TILE_SUB, LANES), F32),
        grid_spec=pltpu.PrefetchScalarGridSpec(
            num_scalar_prefetch=3,
            grid=(n_steps,),
            in_specs=[pl.BlockSpec(memory_space=pl.ANY),
                      wspec((D_MODEL, d_ff)), wspec((1, d_ff)),
                      wspec((D_MODEL, d_ff)), wspec((1, d_ff)),
                      wspec((d_ff, D_MODEL)), wspec((1, D_MODEL))],
            out_specs=pl.BlockSpec(memory_space=pl.ANY),
            scratch_shapes=[pltpu.VMEM(tiles, F32)] * (2 * FFN_BUFS) + [
                pltpu.VMEM((D_MODEL, d_ff), BF16), pltpu.VMEM((D_MODEL, d_ff), BF16),
                pltpu.VMEM((d_ff, D_MODEL), BF16),
                pltpu.SemaphoreType.DMA((FFN_BUFS,)), pltpu.SemaphoreType.DMA((FFN_BUFS,))]),
        compiler_params=_cparams(("arbitrary",)),
        name="moe_ffn",
    )(asg, block_expert, n_used, h2t, w_gate, b_gate.reshape(N_EXPERTS, 1, d_ff),
      w_up, b_up.reshape(N_EXPERTS, 1, d_ff), w_down, b_down.reshape(N_EXPERTS, 1, D_MODEL))


def _combine_kernel(e_ref, x1_ref, gate_ref, g_ref, y_ref):
    tm = x1_ref.shape[0]
    gates = gate_ref[...]
    acc = x1_ref[...]
    for kk in range(TOP_K):
        yk = jnp.concatenate([e_ref[pl.ds(kk * TILE_SUB + c, tm, stride=TOP_K * TILE_SUB), :]
                              for c in range(TILE_SUB)], axis=1)
        acc = acc + yk * gates[:, kk:kk + 1]
    y_ref[...] = _rms(acc, g_ref[...])


def _combine(yt, x1, gates, g_final):
    n = x1.shape[0]
    tm = _pick_tile(n, 8, 256)
    return pl.pallas_call(
        _combine_kernel,
        out_shape=jax.ShapeDtypeStruct((n, D_MODEL), F32),
        grid=(n // tm,),
        in_specs=[pl.BlockSpec((tm * TOP_K * TILE_SUB, LANES), lambda i: (i, 0)),
                  pl.BlockSpec((tm, D_MODEL), lambda i: (i, 0)),
                  pl.BlockSpec((tm, LANES), lambda i: (i, 0)),
                  pl.BlockSpec((1, D_MODEL), lambda i: (0, 0))],
        out_specs=pl.BlockSpec((tm, D_MODEL), lambda i: (i, 0)),
        compiler_params=_cparams(("parallel",)),
        name="moe_combine",
    )(yt, x1, gates, g_final)


def _routing(idx, rank, counts):
    n = idx.shape[0]
    n_assign = n * TOP_K
    flat_e = idx[:, :TOP_K].reshape(-1)
    rank = rank[:, :TOP_K].reshape(-1)
    counts = counts[0, :N_EXPERTS].astype(jnp.int32)
    padded = (counts + MOE_ROWS - 1) // MOE_ROWS * MOE_ROWS
    pad_end = jnp.cumsum(padded)
    pad_start = pad_end - padded
    n_blocks = (n_assign + N_EXPERTS * (MOE_ROWS - 1) + MOE_ROWS - 1) // MOE_ROWS
    dest = (pad_start[flat_e] + rank).astype(jnp.int32) + MOE_ROWS
    tab_rows = n_blocks + 1 + FFN_BUFS
    q = jnp.arange(tab_rows * MOE_ROWS, dtype=jnp.int32) - MOE_ROWS
    e_of = jnp.sum((q[:, None] >= pad_end[None, :]).astype(jnp.int32), axis=1)
    waste = padded - counts
    waste_before = jnp.concatenate([jnp.zeros((1,), waste.dtype), jnp.cumsum(waste)])
    e_c = jnp.minimum(e_of, N_EXPERTS - 1)
    in_expert = MOE_ROWS + waste_before[e_c] + (q - pad_start[e_c] - counts[e_c])
    after = MOE_ROWS + waste_before[N_EXPERTS] + (q - pad_end[N_EXPERTS - 1])
    dump = jnp.where(q < 0, q + MOE_ROWS, jnp.where(e_of < N_EXPERTS, in_expert, after))
    asg = (n_assign + dump).astype(jnp.int32).at[dest].set(jnp.arange(n_assign, dtype=jnp.int32))
    n_used = (pad_end[-1] // MOE_ROWS).astype(jnp.int32)
    blk_start = jnp.arange(n_blocks + 1, dtype=jnp.int32) * MOE_ROWS
    block_expert = jnp.minimum(jnp.sum((blk_start[:, None] >= pad_end[None, :]).astype(jnp.int32), axis=1),
                               N_EXPERTS - 1)
    last_used = block_expert[jnp.maximum(n_used - 1, 0)]
    block_expert = jnp.where(jnp.arange(n_blocks + 1) < n_used, block_expert, last_used).astype(jnp.int32)
    return asg.reshape(tab_rows, MOE_ROWS), block_expert, n_used.reshape(1)


def _forward(x_prompt, x_sample, state_wkv, state_shift, state_conv, meta_tokens, norm_mix, w_in,
             tshift_mu, decay_w0, decay_w2, iclr_a0, iclr_a2, gate_g2, k_k, k_a, r_k, lnx_w, lnx_b,
             glu_b, dw_weight, dw_bias, conv_ln_w, conv_ln_b, w_out, norm_ffn, w_router, b_router,
             w_gate, b_gate, w_up, b_up, w_down, b_down, norm_final):
    n_p, seq, _ = x_prompt.shape
    n_s, t_s, _ = x_sample.shape
    t_real = N_META + seq
    n_pad = (-t_real) % CHUNK
    t_p = t_real + n_pad
    rows_p = n_p * t_p
    rows_s = n_s * t_s
    lyr = 0

    meta = jnp.broadcast_to(meta_tokens.astype(F32)[None], (n_p, N_META, D_MODEL))
    xp = jnp.concatenate([jnp.zeros((n_p, n_pad, D_MODEL), F32), meta, x_prompt], axis=1)
    x_all = jnp.concatenate([xp.reshape(rows_p, D_MODEL), x_sample.reshape(rows_s, D_MODEL)], axis=0)

    row = lambda a: a[lyr].reshape(1, -1).astype(F32)
    w_in_bf = w_in[lyr].astype(BF16)
    p_a, p_b = _in_proj(x_all, row(norm_mix), w_in_bf[:, :SHIFT_COLS], w_in_bf[:, SHIFT_COLS:])

    w_wa = jnp.zeros((LORA_WA, 2 * C_A), F32)
    w_wa = w_wa.at[:D_DECAY_LORA, :C_A].set(decay_w2[lyr]).at[D_DECAY_LORA:, C_A:].set(iclr_a2[lyr])
    prep_w = (row(tshift_mu), row(decay_w0), row(iclr_a0), w_wa.astype(BF16), gate_g2[lyr].astype(BF16),
              row(k_k), row(k_a))
    st_p = _rwkv_prep_prompt(p_a, 0, n_p, t_p, prep_w)
    st_s = _rwkv_prep_sample(p_a, rows_p, n_s, t_s, state_shift[lyr], prep_w)

    scan_w = (row(lnx_w), row(lnx_b), row(r_k))
    ya_p, wkv_p = _wkv_scan(st_p, None, n_p, t_p, CHUNK, 2, 2, *scan_w)
    ya_s, wkv_s = _wkv_scan(st_s, state_wkv[lyr], n_s, t_s, t_s, _pick_tile(n_s, 2, 16), 2, *scan_w)

    conv_w = (row(glu_b), dw_weight[lyr], row(dw_bias), row(conv_ln_w), row(conv_ln_b))
    yb_p, conv_p = _conv_prompt(p_b, 0, n_p, t_p, n_pad, conv_w)
    yb_s, conv_s = _conv_sample(p_b, rows_p, n_s, t_s, state_conv[lyr], conv_w)

    w_out_bf = w_out[lyr].astype(BF16)
    wr = jnp.zeros((D_MODEL, LANES), F32).at[:, :N_EXPERTS].set(w_router[lyr])
    wr_hi = wr.astype(BF16)
    wr_lo = (wr - wr_hi.astype(F32)).astype(BF16)
    b_r = jnp.zeros((1, LANES), F32).at[0, :N_EXPERTS].set(b_router[lyr])
    x1, h2, idx, gates, rank, counts = _out_route(ya_p, ya_s, yb_p, yb_s, x_all, w_out_bf[:C_A], w_out_bf[C_A:],
                                                  row(norm_ffn), wr_hi, wr_lo, b_r)

    asg, block_expert, n_used = _routing(idx, rank, counts)
    yt = _moe_ffn(asg, block_expert, n_used, h2, w_gate[lyr], b_gate[lyr], w_up[lyr], b_up[lyr],
                  w_down[lyr], b_down[lyr])
    y = _combine(yt, x1, gates, norm_final.reshape(1, -1))

    y_prompt = y[:rows_p].reshape(n_p, t_p, D_MODEL)[:, n_pad + N_META:]
    y_sample = y[rows_p:].reshape(n_s, t_s, D_MODEL)
    shift_p = p_a[t_p - 1:rows_p:t_p]
    shift_s = p_a[rows_p + t_s - 1::t_s]
    return (y_prompt.astype(x_prompt.dtype), y_sample.astype(x_sample.dtype),
            wkv_p[None], shift_p[None], conv_p[None], wkv_s[None], shift_s[None], conv_s[None])


def kernel(x_prompt, x_sample, state_wkv, state_shift, state_conv, meta_tokens, norm_mix, w_in, tshift_mu, decay_w0, decay_w2, iclr_a0, iclr_a2, gate_g2, k_k, k_a, r_k, lnx_w, lnx_b, glu_b, dw_weight, dw_bias, conv_ln_w, conv_ln_b, w_out, norm_ffn, w_router, b_router, w_gate, b_gate, w_up, b_up, w_down, b_down, norm_final):
    assert w_in.shape[0] == 1, "single trunk layer"
    return _forward(x_prompt, x_sample, state_wkv, state_shift, state_conv, meta_tokens, norm_mix, w_in,
                    tshift_mu, decay_w0, decay_w2, iclr_a0, iclr_a2, gate_g2, k_k, k_a, r_k, lnx_w, lnx_b,
                    glu_b, dw_weight, dw_bias, conv_ln_w, conv_ln_b, w_out, norm_ffn, w_router, b_router,
                    w_gate, b_gate, w_up, b_up, w_down, b_down, norm_final)
```

```python
import functools

import jax
import jax.numpy as jnp
from jax import lax
from jax.experimental import pallas as pl
from jax.experimental.pallas import tpu as pltpu

F32 = jnp.float32
BF16 = jnp.bfloat16

D_MODEL = 1024
N_META = 16
C_A = 512
HEAD = 64
N_HEADS = C_A // HEAD
C_B = 512
CONV_W = 31
D_DECAY_LORA = 64
D_AAA_LORA = 64
D_GATE_LORA = 128
LORA_WA = D_DECAY_LORA + D_AAA_LORA
SHIFT_COLS = 3 * C_A + LORA_WA + D_GATE_LORA
N_EXPERTS = 32
TOP_K = 4
SWIGLU_LIMIT = 7.0
SWIGLU_ALPHA = 1.702
RMS_EPS = 1e-5
LN_EPS = 1e-5
GN_EPS = 64e-5

LANES = 128
CHUNK = 64
SCAN_UNROLL = 2
CONV_HALO = 32
ROW_TILE = 512
MOE_ROWS = 256
VMEM_LIMIT = 56 * 1024 * 1024


def _cparams(sem):
    return pltpu.CompilerParams(dimension_semantics=sem, vmem_limit_bytes=VMEM_LIMIT)


def _dot(a, b):
    return jnp.dot(a.astype(BF16), b.astype(BF16), preferred_element_type=F32)


def _dot_nt(a, b):
    return lax.dot_general(a.astype(BF16), b.astype(BF16), (((1,), (1,)), ((), ())),
                           preferred_element_type=F32)


def _dot_tn(a, b):
    return lax.dot_general(a.astype(BF16), b.astype(BF16), (((0,), (0,)), ((), ())),
                           preferred_element_type=F32)


def _split2(x):
    hi = x.astype(BF16)
    lo = (x - hi.astype(F32)).astype(BF16)
    return hi, lo


def _split3(x):
    hi = x.astype(BF16)
    r1 = x - hi.astype(F32)
    mid = r1.astype(BF16)
    lo = (r1 - mid.astype(F32)).astype(BF16)
    return hi, mid, lo


def _dot_exact_rhs(x, m_bf16):
    hi, mid, lo = _split3(x)
    d = functools.partial(jnp.dot, preferred_element_type=F32)
    return d(hi, m_bf16) + d(mid, m_bf16) + d(lo, m_bf16)


def _dot_exact_lhs(m_bf16, x):
    hi, mid, lo = _split3(x)
    d = functools.partial(jnp.dot, preferred_element_type=F32)
    return d(m_bf16, hi) + d(m_bf16, mid) + d(m_bf16, lo)


def _rms(x, g):
    return x * lax.rsqrt(jnp.mean(x * x, axis=-1, keepdims=True) + RMS_EPS) * g


def _sigmoid(x):
    return 1.0 / (1.0 + jnp.exp(-x))


def _head_ones():
    i = lax.broadcasted_iota(jnp.int32, (C_A, C_A), 0) // HEAD
    j = lax.broadcasted_iota(jnp.int32, (C_A, C_A), 1) // HEAD
    return (i == j).astype(BF16)


def _pick_tile(n, mult, cap):
    best = mult
    t = mult
    while t <= min(n, cap):
        if n % t == 0:
            best = t
        t += mult
    assert n % best == 0, (n, mult)
    return best


def _in_proj_kernel(x_ref, g_ref, wa_ref, wb_ref, pa_ref, pb_ref):
    h = _rms(x_ref[...], g_ref[...]).astype(BF16)
    pa_ref[...] = jnp.dot(h, wa_ref[...], preferred_element_type=F32)
    pb_ref[...] = jnp.dot(h, wb_ref[...], preferred_element_type=F32)


def _in_proj(x, g, w_a, w_b):
    n = x.shape[0]
    tm = _pick_tile(n, 8, ROW_TILE)
    return pl.pallas_call(
        _in_proj_kernel,
        out_shape=(jax.ShapeDtypeStruct((n, SHIFT_COLS), F32), jax.ShapeDtypeStruct((n, 2 * C_B), F32)),
        grid=(n // tm,),
        in_specs=[pl.BlockSpec((tm, D_MODEL), lambda i: (i, 0)),
                  pl.BlockSpec((1, D_MODEL), lambda i: (0, 0)),
                  pl.BlockSpec((D_MODEL, SHIFT_COLS), lambda i: (0, 0)),
                  pl.BlockSpec((D_MODEL, 2 * C_B), lambda i: (0, 0))],
        out_specs=(pl.BlockSpec((tm, SHIFT_COLS), lambda i: (i, 0)),
                   pl.BlockSpec((tm, 2 * C_B), lambda i: (i, 0))),
        compiler_params=_cparams(("parallel",)),
        name="in_proj",
    )(x, g, w_a, w_b)


def _prep_math(p, pprev, mu, w0, a0, w_wa, w_g, k_k, k_a, ones_h, outs):
    r_ref, k_ref, v_ref, kk_ref, b_ref, lw_ref, g_ref = outs
    xs = p + (pprev - p) * mu
    r = xs[:, 0:C_A]
    k = xs[:, C_A:2 * C_A]
    v = xs[:, 2 * C_A:3 * C_A]
    lo = xs[:, 3 * C_A:3 * C_A + LORA_WA]
    g_lo = xs[:, 3 * C_A + LORA_WA:]
    lane = lax.broadcasted_iota(jnp.int32, lo.shape, 1)
    lo = jnp.where(lane < D_DECAY_LORA, jnp.tanh(lo), lo)
    wa = _dot(lo, w_wa)
    z = -(w0 + wa[:, :C_A])
    w = -(jnp.maximum(z, 0.0) + jnp.log(1.0 + jnp.exp(-jnp.abs(z)))) - 0.5
    lw_ref[...] = -jnp.exp(w)
    a = _sigmoid(a0 + wa[:, C_A:])
    g_ref[...] = _dot(_sigmoid(g_lo), w_g)
    kk = k * k_k
    hi, lo2 = _split2(kk * kk)
    ss = jnp.dot(hi, ones_h, preferred_element_type=F32) + jnp.dot(lo2, ones_h, preferred_element_type=F32)
    kk = kk / jnp.maximum(jnp.sqrt(ss), 1e-12)
    r_ref[...] = r
    v_ref[...] = v
    k_ref[...] = k * (1.0 + (a - 1.0) * k_a)
    kk_ref[...] = kk
    b_ref[...] = kk * a


def _prep_prompt_kernel(p_ref, halo_ref, mu_ref, w0_ref, a0_ref, wwa_ref, wg_ref, kk_ref_w, ka_ref, *outs):
    p = p_ref[...]
    prev_last = jnp.where(pl.program_id(1) == 0, 0.0, halo_ref[7:8, :])
    rolled = pltpu.roll(p, 1, 0)
    row = lax.broadcasted_iota(jnp.int32, p.shape, 0)
    pprev = jnp.where(row == 0, prev_last, rolled)
    _prep_math(p, pprev, mu_ref[...], w0_ref[...], a0_ref[...], wwa_ref[...], wg_ref[...],
               kk_ref_w[...], ka_ref[...], _head_ones(), outs)


def _prep_sample_kernel(p_ref, shift_ref, mu_ref, w0_ref, a0_ref, wwa_ref, wg_ref, kk_ref_w, ka_ref,
                        *outs_and_scratch):
    outs = outs_and_scratch
    p = p_ref[...]
    sb = shift_ref.shape[0]
    t_len = p.shape[0] // sb
    first = jnp.broadcast_to(shift_ref[...][:, None, :], (sb, t_len, SHIFT_COLS)).reshape(p.shape)
    row = lax.broadcasted_iota(jnp.int32, p.shape, 0)
    pprev = jnp.where(row % t_len == 0, first, pltpu.roll(p, 1, 0))
    _prep_math(p, pprev, mu_ref[...], w0_ref[...], a0_ref[...], wwa_ref[...], wg_ref[...],
               kk_ref_w[...], ka_ref[...], _head_ones(), outs)


def _prep_weight_specs(nd):
    z = (lambda *_: (0, 0))
    del nd
    return [pl.BlockSpec((1, SHIFT_COLS), z), pl.BlockSpec((1, C_A), z), pl.BlockSpec((1, C_A), z),
            pl.BlockSpec((LORA_WA, 2 * C_A), z), pl.BlockSpec((D_GATE_LORA, C_A), z),
            pl.BlockSpec((1, C_A), z), pl.BlockSpec((1, C_A), z)]


def _rwkv_prep_prompt(p_a, row0, n_seq, t_len, wts):
    tm = _pick_tile(t_len, 8, 704)
    nt = t_len // tm
    assert row0 % tm == 0
    base = row0 // tm
    out = jax.ShapeDtypeStruct((n_seq * t_len, C_A), F32)
    ospec = pl.BlockSpec((tm, C_A), lambda b, c: (b * nt + c, 0))
    return pl.pallas_call(
        _prep_prompt_kernel,
        out_shape=(out,) * 7,
        grid=(n_seq, nt),
        in_specs=[pl.BlockSpec((tm, SHIFT_COLS), lambda b, c: (base + b * nt + c, 0)),
                  pl.BlockSpec((8, SHIFT_COLS),
                               lambda b, c: (jnp.maximum((base + b * nt + c) * (tm // 8) - 1, 0), 0)),
                  ] + _prep_weight_specs(2),
        out_specs=(ospec,) * 7,
        compiler_params=_cparams(("parallel", "parallel")),
        name="rwkv_prep_prompt",
    )(p_a, p_a, *wts)


def _rwkv_prep_sample(p_a, row0, n_seq, t_len, shift, wts):
    sb = _pick_tile(n_seq, 8, 64)
    tm = sb * t_len
    assert row0 % tm == 0 and t_len % 8 == 0
    base = row0 // tm
    out = jax.ShapeDtypeStruct((n_seq * t_len, C_A), F32)
    ospec = pl.BlockSpec((tm, C_A), lambda i: (i, 0))
    return pl.pallas_call(
        _prep_sample_kernel,
        out_shape=(out,) * 7,
        grid=(n_seq // sb,),
        in_specs=[pl.BlockSpec((tm, SHIFT_COLS), lambda i: (base + i, 0)),
                  pl.BlockSpec((sb, SHIFT_COLS), lambda i: (i, 0)),
                  ] + _prep_weight_specs(1),
        out_specs=(ospec,) * 7,
        compiler_params=_cparams(("parallel",)),
        name="rwkv_prep_sample",
    )(p_a, shift, *wts)


def _scan_kernel(has_s0, unroll, *refs):
    if has_s0:
        (r_ref, k_ref, v_ref, kk_ref, b_ref, lw_ref, g_ref, s0_ref,
         lnw_ref, lnb_ref, rk_ref, y_ref, s_ref) = refs
    else:
        (r_ref, k_ref, v_ref, kk_ref, b_ref, lw_ref, g_ref,
         lnw_ref, lnb_ref, rk_ref, y_ref, s_ref) = refs
        s0_ref = None
    n_seq, c_len, _ = r_ref.shape

    @pl.when(pl.program_id(1) == 0)
    def _():
        if has_s0:
            s_ref[...] = s0_ref[...]
        else:
            s_ref[...] = jnp.zeros_like(s_ref)

    ti = lax.broadcasted_iota(jnp.int32, (c_len, c_len), 0)
    si = lax.broadcasted_iota(jnp.int32, (c_len, c_len), 1)
    tri_incl = (si <= ti).astype(BF16)
    strict = (si < ti).astype(F32)
    incl = (si <= ti).astype(F32)
    incl_signed = jnp.concatenate([incl, -incl], axis=1)
    ones_h = _head_ones()
    lnw, lnb, rk = lnw_ref[...], lnb_ref[...], rk_ref[...]
    heads = range(N_HEADS)
    hsl = [slice(h * HEAD, (h + 1) * HEAD) for h in heads]

    def gsum(x):
        hi, lo = _split2(x)
        return jnp.dot(hi, ones_h, preferred_element_type=F32) + jnp.dot(lo, ones_h, preferred_element_type=F32)

    def load(s):
        return ([ref[s] for ref in (r_ref, k_ref, v_ref, kk_ref, b_ref, lw_ref, g_ref)],
                [s_ref[s, h] for h in heads])

    def compute(streams, states):
        r, k, v, kk, b, lw, g = streams
        cum = _dot_exact_lhs(tri_incl, lw)
        tot = cum[c_len - 1:c_len, :]
        g_inv = jnp.exp(-cum)
        g_end = jnp.exp(tot - cum)
        l_mat = jnp.concatenate([kk * jnp.exp(cum - lw), r * jnp.exp(cum)], axis=0).astype(BF16)
        kbh = jnp.concatenate([k * g_inv, b * g_inv], axis=0).astype(BF16)
        e_mat = jnp.concatenate([k * g_end, -(b * g_end)], axis=0).astype(BF16)
        g_tot = jnp.exp(tot)
        l_h = [l_mat[:, hs] for hs in hsl]
        v_h = [v[:, hs] for hs in hsl]
        n_ub = [_dot_nt(l_h[h][:c_len], kbh[c_len:, hsl[h]]) * strict for h in heads]
        a_vk = [_dot_nt(l_h[h][:c_len], kbh[:c_len, hsl[h]]) * strict for h in heads]
        pm = [_dot_nt(l_h[h], states[h]) for h in heads]
        x = [pm[h][:c_len] + _dot(a_vk[h], v_h[h]) for h in heads]
        x = [x[h] - _dot(n_ub[h], x[h]) for h in heads]
        pw = n_ub
        m = 2
        while m < c_len:
            pw = [_dot(p, p) for p in pw]
            x = [x[h] + _dot(pw[h], x[h]) for h in heads]
            m *= 2
        b_m = [_dot_nt(l_h[h][c_len:], kbh[:, hsl[h]]) * incl_signed for h in heads]
        vu = [jnp.concatenate([v_h[h], x[h]], axis=0) for h in heads]
        y = jnp.concatenate([pm[h][c_len:] + _dot(b_m[h], vu[h]) for h in heads], axis=1)
        new_states = [states[h] * g_tot[:, hsl[h]] + _dot_tn(vu[h], e_mat[:, hsl[h]]) for h in heads]
        mu = gsum(y) * (1.0 / HEAD)
        d = y - mu
        var = gsum(d * d) * (1.0 / HEAD)
        yn = d * lax.rsqrt(var + GN_EPS) * lnw + lnb
        bonus = gsum(r * k * rk) * v
        return (yn + bonus) * g, new_states

    def group(base):
        idx = [base + i for i in range(unroll)]
        loaded = [load(s) for s in idx]
        results = [compute(*d) for d in loaded]
        for s, (y, new_states) in zip(idx, results):
            y_ref[s] = y
            for h in heads:
                s_ref[s, h] = new_states[h]

    if n_seq == unroll:
        group(0)
    else:
        def body(i, carry):
            group(i * unroll)
            return carry
        lax.fori_loop(0, n_seq // unroll, body, 0)


def _wkv_scan(streams, s0, n_seq, t_len, c_len, sb, unroll, lnw, lnb, rk):
    nc = t_len // c_len
    streams = [a.reshape(n_seq, t_len, C_A) for a in streams]
    dspec = pl.BlockSpec((sb, c_len, C_A), lambda i, c: (i, c, 0))
    sspec = pl.BlockSpec((sb, N_HEADS, HEAD, HEAD), lambda i, c: (i, 0, 0, 0))
    wspec = pl.BlockSpec((1, C_A), lambda i, c: (0, 0))
    ins = list(streams) + ([s0] if s0 is not None else []) + [lnw, lnb, rk]
    in_specs = [dspec] * 7 + ([sspec] if s0 is not None else []) + [wspec] * 3
    y, s_new = pl.pallas_call(
        functools.partial(_scan_kernel, s0 is not None, unroll),
        out_shape=(jax.ShapeDtypeStruct((n_seq, t_len, C_A), F32),
                   jax.ShapeDtypeStruct((n_seq, N_HEADS, HEAD, HEAD), F32)),
        grid=(n_seq // sb, nc),
        in_specs=in_specs,
        out_specs=(dspec, sspec),
        compiler_params=_cparams(("parallel", "arbitrary")),
        name="wkv_scan_c%d" % c_len,
    )(*ins)
    return y.reshape(n_seq * t_len, C_A), s_new


def _conv_tail(z, dwb, lnw, lnb):
    z = z + dwb
    mu = jnp.mean(z, axis=-1, keepdims=True)
    d = z - mu
    var = jnp.mean(d * d, axis=-1, keepdims=True)
    z = d * lax.rsqrt(var + LN_EPS) * lnw + lnb
    return z * _sigmoid(z)


def _glu(pb, glu_b):
    u = pb + glu_b
    return u[:, :C_B] * _sigmoid(u[:, C_B:])


def _conv_prompt_kernel(n_pad, pb_ref, halo_ref, glub_ref, dw_ref, dwb_ref, lnw_ref, lnb_ref,
                        y_ref, st_ref, ext_ref):
    c = pl.program_id(1)
    tm = pb_ref.shape[0]
    glub = glub_ref[...]
    halo = jnp.where(c == 0, 0.0, _glu(halo_ref[...], glub))
    u = _glu(pb_ref[...], glub)
    t_glob = c * tm + lax.broadcasted_iota(jnp.int32, u.shape, 0)
    u = jnp.where(t_glob < n_pad, 0.0, u)
    ext_ref[0:CONV_HALO, :] = halo
    ext_ref[CONV_HALO:, :] = u
    off = CONV_HALO - (CONV_W - 1)
    rb = 32

    def blk(i, carry):
        base = pl.multiple_of(i * rb, rb)
        win = ext_ref[pl.ds(base, rb + CONV_HALO), :]
        acc = jnp.zeros((rb, C_B), F32)
        for s in range(8):
            taps = [j for j in range(CONV_W) if (off + j) % 8 == s]
            shifted = win if s == 0 else pltpu.roll(win, win.shape[0] - s, 0)
            for j in taps:
                a = off + j - s
                acc = acc + shifted[a:a + rb] * dw_ref[j:j + 1, :]
        y_ref[pl.ds(base, rb), :] = _conv_tail(acc, dwb_ref[...], lnw_ref[...], lnb_ref[...])
        return carry

    lax.fori_loop(0, tm // rb, blk, 0)

    @pl.when(c == pl.num_programs(1) - 1)
    def _():
        st_ref[0] = ext_ref[CONV_HALO + tm - (CONV_W - 1):CONV_HALO + tm, :]


def _conv_prompt(p_b, row0, n_seq, t_len, n_pad, wts):
    tm = _pick_tile(t_len, CONV_HALO, 704)
    nt = t_len // tm
    assert row0 % tm == 0
    base = row0 // tm
    z2 = lambda b, c: (0, 0)
    return pl.pallas_call(
        functools.partial(_conv_prompt_kernel, n_pad),
        out_shape=(jax.ShapeDtypeStruct((n_seq * t_len, C_B), F32),
                   jax.ShapeDtypeStruct((n_seq, CONV_W - 1, C_B), F32)),
        grid=(n_seq, nt),
        in_specs=[pl.BlockSpec((tm, 2 * C_B), lambda b, c: (base + b * nt + c, 0)),
                  pl.BlockSpec((CONV_HALO, 2 * C_B),
                               lambda b, c: (jnp.maximum((base + b * nt + c) * (tm // CONV_HALO) - 1, 0), 0)),
                  pl.BlockSpec((1, 2 * C_B), z2), pl.BlockSpec((CONV_W, C_B), z2),
                  pl.BlockSpec((1, C_B), z2), pl.BlockSpec((1, C_B), z2), pl.BlockSpec((1, C_B), z2)],
        out_specs=(pl.BlockSpec((tm, C_B), lambda b, c: (b * nt + c, 0)),
                   pl.BlockSpec((1, CONV_W - 1, C_B), lambda b, c: (b, 0, 0))),
        scratch_shapes=[pltpu.VMEM((CONV_HALO + tm, C_B), F32)],
        compiler_params=_cparams(("parallel", "arbitrary")),
        name="conv_prompt",
    )(p_b, p_b, *wts)


def _conv_sample_kernel(pb_ref, buf_ref, glub_ref, dw_ref, dwb_ref, lnw_ref, lnb_ref,
                        y_ref, st_ref, ext_ref):
    sb, t_len = buf_ref.shape[0], pb_ref.shape[0] // buf_ref.shape[0]
    hist = CONV_W - 1
    u = _glu(pb_ref[...], glub_ref[...])
    lead = ext_ref.shape[1] - hist - t_len
    ext_ref[:, lead:lead + hist, :] = buf_ref[...]
    ext_ref[:, lead + hist:, :] = u.reshape(sb, t_len, C_B)
    acc = jnp.zeros((sb, t_len, C_B), F32)
    for j in range(CONV_W):
        acc = acc + ext_ref[:, lead + j:lead + j + t_len, :] * dw_ref[j:j + 1, :]
    z = _conv_tail(acc.reshape(sb * t_len, C_B), dwb_ref[...], lnw_ref[...], lnb_ref[...])
    y_ref[...] = z
    st_ref[...] = ext_ref[:, lead + t_len:, :]


def _conv_sample(p_b, row0, n_seq, t_len, buf, wts):
    sb = _pick_tile(n_seq, 8, 32)
    tm = sb * t_len
    assert row0 % tm == 0 and t_len % 8 == 0
    base = row0 // tm
    hist = CONV_W - 1
    ext_rows = -(-(hist + t_len) // 8) * 8
    z1 = lambda i: (0, 0)
    return pl.pallas_call(
        _conv_sample_kernel,
        out_shape=(jax.ShapeDtypeStruct((n_seq * t_len, C_B), F32),
                   jax.ShapeDtypeStruct((n_seq, hist, C_B), F32)),
        grid=(n_seq // sb,),
        in_specs=[pl.BlockSpec((tm, 2 * C_B), lambda i: (base + i, 0)),
                  pl.BlockSpec((sb, hist, C_B), lambda i: (i, 0, 0)),
                  pl.BlockSpec((1, 2 * C_B), z1), pl.BlockSpec((CONV_W, C_B), z1),
                  pl.BlockSpec((1, C_B), z1), pl.BlockSpec((1, C_B), z1), pl.BlockSpec((1, C_B), z1)],
        out_specs=(pl.BlockSpec((tm, C_B), lambda i: (i, 0)),
                   pl.BlockSpec((sb, hist, C_B), lambda i: (i, 0, 0))),
        scratch_shapes=[pltpu.VMEM((sb, ext_rows, C_B), F32)],
        compiler_params=_cparams(("parallel",)),
        name="conv_sample",
    )(p_b, buf, *wts)


TILE_SUB = D_MODEL // LANES


def _store_token_tiles(ref, x, lead=()):
    rows = x.shape[0]
    for c in range(TILE_SUB):
        ref[lead + (pl.ds(c, rows, stride=TILE_SUB), slice(None))] = x[:, c * LANES:(c + 1) * LANES]


def _load_token_tiles(ref, rows, lead=()):
    return jnp.concatenate([ref[lead + (pl.ds(c, rows, stride=TILE_SUB), slice(None))]
                            for c in range(TILE_SUB)], axis=1)


def _out_route_kernel(n_first, yaf_ref, yas_ref, ybf_ref, ybs_ref, x_ref, woa_ref, wob_ref, g_ref,
                      wrh_ref, wrl_ref, br_ref, x1_ref, h2_ref, idx_ref, gate_ref, rank_ref, cnt_ref, run_ref):
    i = pl.program_id(0)
    first = i < n_first
    ya = jnp.where(first, yaf_ref[...], yas_ref[...])
    yb = jnp.where(first, ybf_ref[...], ybs_ref[...])
    x1 = x_ref[...] + _dot(ya, woa_ref[...]) + _dot(yb, wob_ref[...])
    x1_ref[...] = x1
    h2 = _rms(x1, g_ref[...])
    _store_token_tiles(h2_ref, h2)
    hi, lo = _split2(h2)
    d = functools.partial(jnp.dot, preferred_element_type=F32)
    logits = d(hi, wrh_ref[...]) + d(hi, wrl_ref[...]) + d(lo, wrh_ref[...]) + br_ref[...]
    lane = lax.broadcasted_iota(jnp.int32, logits.shape, 1)
    logits = jnp.where(lane < N_EXPERTS, logits, -jnp.inf)
    idx_out = jnp.zeros(logits.shape, jnp.int32)
    val_out = jnp.full(logits.shape, -jnp.inf, F32)
    onehots = []
    for kk in range(TOP_K):
        m = jnp.max(logits, axis=-1, keepdims=True)
        sel = jnp.min(jnp.where(logits == m, lane, LANES), axis=-1, keepdims=True)
        idx_out = jnp.where(lane == kk, sel, idx_out)
        val_out = jnp.where(lane == kk, m, val_out)
        hit = lane == sel
        onehots.append(hit.astype(F32))
        logits = jnp.where(hit, -jnp.inf, logits)
    e = jnp.exp(val_out - jnp.max(val_out, axis=-1, keepdims=True))
    idx_ref[...] = idx_out
    gate_ref[...] = e / jnp.sum(e, axis=-1, keepdims=True)

    @pl.when(i == 0)
    def _():
        run_ref[...] = jnp.zeros_like(run_ref)

    tm = logits.shape[0]
    total = onehots[0] + onehots[1] + onehots[2] + onehots[3]
    ti = lax.broadcasted_iota(jnp.int32, (tm, tm), 0)
    si = lax.broadcasted_iota(jnp.int32, (tm, tm), 1)
    before = jnp.dot((si < ti).astype(BF16), total.astype(BF16), preferred_element_type=F32) + run_ref[...]
    rank = jnp.zeros(logits.shape, F32)
    for kk in range(TOP_K):
        rank = jnp.where(lane == kk, jnp.sum(onehots[kk] * before, axis=-1, keepdims=True), rank)
    rank_ref[...] = rank.astype(jnp.int32)
    run_ref[...] += jnp.sum(total, axis=0, keepdims=True)
    cnt_ref[...] = run_ref[...]


def _out_route(ya_f, ya_s, yb_f, yb_s, x, wo_a, wo_b, g, wr_hi, wr_lo, b_r):
    n = x.shape[0]
    n_f = ya_f.shape[0]
    tm = ROW_TILE
    while n_f % tm or (n - n_f) % tm:
        tm //= 2
    assert tm % 8 == 0
    nf_t, ns_t = n_f // tm, (n - n_f) // tm
    z = lambda i: (0, 0)
    row = lambda w: pl.BlockSpec((tm, w), lambda i: (i, 0))
    fst = lambda w: pl.BlockSpec((tm, w), lambda i: (jnp.minimum(i, nf_t - 1), 0))
    snd = lambda w: pl.BlockSpec((tm, w), lambda i: (jnp.maximum(i - nf_t, 0), 0))
    return pl.pallas_call(
        functools.partial(_out_route_kernel, nf_t),
        out_shape=(jax.ShapeDtypeStruct((n, D_MODEL), F32), jax.ShapeDtypeStruct((n * TILE_SUB, LANES), F32),
                   jax.ShapeDtypeStruct((n, LANES), jnp.int32), jax.ShapeDtypeStruct((n, LANES), F32),
                   jax.ShapeDtypeStruct((n, LANES), jnp.int32), jax.ShapeDtypeStruct((1, LANES), F32)),
        grid=(nf_t + ns_t,),
        in_specs=[fst(C_A), snd(C_A), fst(C_B), snd(C_B), row(D_MODEL),
                  pl.BlockSpec((C_A, D_MODEL), z), pl.BlockSpec((C_B, D_MODEL), z),
                  pl.BlockSpec((1, D_MODEL), z),
                  pl.BlockSpec((D_MODEL, LANES), z), pl.BlockSpec((D_MODEL, LANES), z),
                  pl.BlockSpec((1, LANES), z)],
        out_specs=(row(D_MODEL), pl.BlockSpec((tm * TILE_SUB, LANES), lambda i: (i, 0)), row(LANES), row(LANES),
                   row(LANES), pl.BlockSpec((1, LANES), z)),
        scratch_shapes=[pltpu.VMEM((1, LANES), F32)],
        compiler_params=_cparams(("arbitrary",)),
        name="out_route",
    )(ya_f, ya_s, yb_f, yb_s, x, wo_a, wo_b, g, wr_hi, wr_lo, b_r)


FFN_BUFS = 3


def _ffn_kernel(n_tok, asg_ref, bexp_ref, nblk_ref, h2_hbm, wg_ref, bg_ref, wu_ref, bu_ref, wd_ref, bd_ref,
                out_hbm, *scratch):
    xbufs, ybufs = scratch[:FFN_BUFS], scratch[FFN_BUFS:2 * FFN_BUFS]
    wg_bf, wu_bf, wd_bf, gsem, ssem = scratch[2 * FFN_BUFS:]
    j = pl.program_id(0)
    last = pl.num_programs(0) - 1
    n_used = nblk_ref[0]
    rows = MOE_ROWS

    def gather(i, tab_row, s):
        tok = jnp.minimum(lax.shift_right_logical(asg_ref[tab_row, i], jnp.int32(2)), n_tok - 1)
        return pltpu.make_async_copy(h2_hbm.at[pl.ds(pl.multiple_of(tok * TILE_SUB, TILE_SUB), TILE_SUB), :],
                                     xbufs[s].at[pl.ds(i * TILE_SUB, TILE_SUB), :], gsem.at[s])

    def scatter(i, tab_row, s):
        dst = asg_ref[tab_row, i]
        return pltpu.make_async_copy(ybufs[s].at[pl.ds(i * TILE_SUB, TILE_SUB), :],
                                     out_hbm.at[pl.ds(pl.multiple_of(dst * TILE_SUB, TILE_SUB), TILE_SUB), :],
                                     ssem.at[s])

    def start_all(fn, unrolled):
        if unrolled:
            for i in range(rows):
                fn(i).start(priority=i % 2)
        else:
            def body(i, carry):
                fn(i).start()
                return carry
            lax.fori_loop(0, rows, body, 0)

    def wait_gather(s):
        pltpu.make_async_copy(h2_hbm.at[pl.ds(0, rows * TILE_SUB), :], xbufs[s], gsem.at[s]).wait()

    def wait_scatter(s):
        pltpu.make_async_copy(ybufs[s], out_hbm.at[pl.ds(0, rows * TILE_SUB), :], ssem.at[s]).wait()

    @pl.when(j == 0)
    def _():
        for yb in ybufs:
            yb[...] = jnp.zeros_like(yb)
        start_all(lambda i: gather(i, 1, 0), False)
        start_all(lambda i: gather(i, 2, 1), False)

    def step(slot):
        prev = (slot + FFN_BUFS - 1) % FFN_BUFS
        nxt = (slot + 1) % FFN_BUFS

        def start_next(unrolled):
            start_all(lambda i: gather(i, j + 3, prev), unrolled)
            start_all(lambda i: scatter(i, j, prev), unrolled)

        @pl.when(j >= 2)
        def _():
            wait_scatter(slot)

        wait_gather(slot)

        @pl.when(j < n_used)
        def _():
            new_expert = jnp.logical_or(j == 0, bexp_ref[j] != bexp_ref[jnp.maximum(j - 1, 0)])

            @pl.when(new_expert)
            def _():
                wg_bf[...] = wg_ref[0].astype(BF16)
                wu_bf[...] = wu_ref[0].astype(BF16)
                wd_bf[...] = wd_ref[0].astype(BF16)

            x = _load_token_tiles(xbufs[slot], rows).astype(BF16)
            start_next(True)
            gt = jnp.minimum(jnp.dot(x, wg_bf[...], preferred_element_type=F32) + bg_ref[0], SWIGLU_LIMIT)
            up = jnp.clip(jnp.dot(x, wu_bf[...], preferred_element_type=F32) + bu_ref[0],
                          -SWIGLU_LIMIT, SWIGLU_LIMIT)
            act = (up + 1.0) * gt * _sigmoid(SWIGLU_ALPHA * gt)
            y = jnp.dot(act.astype(BF16), wd_bf[...], preferred_element_type=F32) + bd_ref[0]
            _store_token_tiles(ybufs[slot], y)

        @pl.when(j >= n_used)
        def _():
            start_next(False)

        @pl.when(j == last)
        def _():
            for s in (nxt, prev):
                wait_gather(s)
                wait_scatter(s)

    for s in range(FFN_BUFS):
        pl.when(lax.rem(j, FFN_BUFS) == s)(functools.partial(step, s))


def _moe_ffn(asg, block_expert, n_used, h2t, w_gate, b_gate, w_up, b_up, w_down, b_down):
    n_steps = block_expert.shape[0]
    n_tok = h2t.shape[0] // TILE_SUB
    d_ff = w_gate.shape[2]
    out_rows = n_steps * MOE_ROWS
    wspec = lambda shape: pl.BlockSpec((1,) + shape, lambda j, a, be, nb: (be[j], 0, 0))
    tiles = (MOE_ROWS * TILE_SUB, LANES)
    return pl.pallas_call(
        functools.partial(_ffn_kernel, n_tok),
        out_shape=jax.ShapeDtypeStruct((out_rows * TILE_SUB, LANES), F32),
        grid_spec=pltpu.PrefetchScalarGridSpec(
            num_scalar_prefetch=3,
            grid=(n_steps,),
            in_specs=[pl.BlockSpec(memory_space=pl.ANY),
                      wspec((D_MODEL, d_ff)), wspec((1, d_ff)),
                      wspec((D_MODEL, d_ff)), wspec((1, d_ff)),
                      wspec((d_ff, D_MODEL)), wspec((1, D_MODEL))],
            out_specs=pl.BlockSpec(memory_space=pl.ANY),
            scratch_shapes=[pltpu.VMEM(tiles, F32)] * (2 * FFN_BUFS) + [
                pltpu.VMEM((D_MODEL, d_ff), BF16), pltpu.VMEM((D_MODEL, d_ff), BF16),
                pltpu.VMEM((d_ff, D_MODEL), BF16),
                pltpu.SemaphoreType.DMA((FFN_BUFS,)), pltpu.SemaphoreType.DMA((FFN_BUFS,))]),
        compiler_params=_cparams(("arbitrary",)),
        name="moe_ffn",
    )(asg, block_expert, n_used, h2t, w_gate, b_gate.reshape(N_EXPERTS, 1, d_ff),
      w_up, b_up.reshape(N_EXPERTS, 1, d_ff), w_down, b_down.reshape(N_EXPERTS, 1, D_MODEL))


def _combine_kernel(e_ref, x1_ref, gate_ref, g_ref, y_ref):
    tm = x1_ref.shape[0]
    gates = gate_ref[...]
    acc = x1_ref[...]
    for kk in range(TOP_K):
        yk = jnp.concatenate([e_ref[pl.ds(kk * TILE_SUB + c, tm, stride=TOP_K * TILE_SUB), :]
                              for c in range(TILE_SUB)], axis=1)
        acc = acc + yk * gates[:, kk:kk + 1]
    y_ref[...] = _rms(acc, g_ref[...])


def _combine(yt, x1, gates, g_final):
    n = x1.shape[0]
    tm = _pick_tile(n, 8, 256)
    return pl.pallas_call(
        _combine_kernel,
        out_shape=jax.ShapeDtypeStruct((n, D_MODEL), F32),
        grid=(n // tm,),
        in_specs=[pl.BlockSpec((tm * TOP_K * TILE_SUB, LANES), lambda i: (i, 0)),
                  pl.BlockSpec((tm, D_MODEL), lambda i: (i, 0)),
                  pl.BlockSpec((tm, LANES), lambda i: (i, 0)),
                  pl.BlockSpec((1, D_MODEL), lambda i: (0, 0))],
        out_specs=pl.BlockSpec((tm, D_MODEL), lambda i: (i, 0)),
        compiler_params=_cparams(("parallel",)),
        name="moe_combine",
    )(yt, x1, gates, g_final)


def _routing(idx, rank, counts):
    n = idx.shape[0]
    n_assign = n * TOP_K
    flat_e = idx[:, :TOP_K].reshape(-1)
    rank = rank[:, :TOP_K].reshape(-1)
    counts = counts[0, :N_EXPERTS].astype(jnp.int32)
    padded = (counts + MOE_ROWS - 1) // MOE_ROWS * MOE_ROWS
    pad_end = jnp.cumsum(padded)
    pad_start = pad_end - padded
    n_blocks = (n_assign + N_EXPERTS * (MOE_ROWS - 1) + MOE_ROWS - 1) // MOE_ROWS
    dest = (pad_start[flat_e] + rank).astype(jnp.int32) + MOE_ROWS
    tab_rows = n_blocks + 1 + FFN_BUFS
    q = jnp.arange(tab_rows * MOE_ROWS, dtype=jnp.int32) - MOE_ROWS
    e_blk = jnp.sum((q[::MOE_ROWS, None] >= pad_end[None, :]).astype(jnp.int32), axis=1)
    e_of = jnp.broadcast_to(e_blk[:, None], (tab_rows, MOE_ROWS)).reshape(-1)
    waste = padded - counts
    waste_before = jnp.concatenate([jnp.zeros((1,), waste.dtype), jnp.cumsum(waste)])
    e_c = jnp.minimum(e_of, N_EXPERTS - 1)
    in_expert = MOE_ROWS + waste_before[e_c] + (q - pad_start[e_c] - counts[e_c])
    after = MOE_ROWS + waste_before[N_EXPERTS] + (q - pad_end[N_EXPERTS - 1])
    dump = jnp.where(q < 0, q + MOE_ROWS, jnp.where(e_of < N_EXPERTS, in_expert, after))
    asg = (n_assign + dump).astype(jnp.int32).at[dest].set(
        jnp.arange(n_assign, dtype=jnp.int32), unique_indices=True, mode='promise_in_bounds')
    n_used = (pad_end[-1] // MOE_ROWS).astype(jnp.int32)
    blk_start = jnp.arange(n_blocks + 1, dtype=jnp.int32) * MOE_ROWS
    block_expert = jnp.minimum(jnp.sum((blk_start[:, None] >= pad_end[None, :]).astype(jnp.int32), axis=1),
                               N_EXPERTS - 1)
    last_used = block_expert[jnp.maximum(n_used - 1, 0)]
    block_expert = jnp.where(jnp.arange(n_blocks + 1) < n_used, block_expert, last_used).astype(jnp.int32)
    return asg.reshape(tab_rows, MOE_ROWS), block_expert, n_used.reshape(1)


def _forward(x_prompt, x_sample, state_wkv, state_shift, state_conv, meta_tokens, norm_mix, w_in,
             tshift_mu, decay_w0, decay_w2, iclr_a0, iclr_a2, gate_g2, k_k, k_a, r_k, lnx_w, lnx_b,
             glu_b, dw_weight, dw_bias, conv_ln_w, conv_ln_b, w_out, norm_ffn, w_router, b_router,
             w_gate, b_gate, w_up, b_up, w_down, b_down, norm_final):
    n_p, seq, _ = x_prompt.shape
    n_s, t_s, _ = x_sample.shape
    t_real = N_META + seq
    n_pad = (-t_real) % CHUNK
    t_p = t_real + n_pad
    rows_p = n_p * t_p
    rows_s = n_s * t_s
    lyr = 0

    meta = jnp.broadcast_to(meta_tokens.astype(F32)[None], (n_p, N_META, D_MODEL))
    xp = jnp.concatenate([jnp.zeros((n_p, n_pad, D_MODEL), F32), meta, x_prompt], axis=1)
    x_all = jnp.concatenate([xp.reshape(rows_p, D_MODEL), x_sample.reshape(rows_s, D_MODEL)], axis=0)

    row = lambda a: a[lyr].reshape(1, -1).astype(F32)
    w_in_bf = w_in[lyr].astype(BF16)
    p_a, p_b = _in_proj(x_all, row(norm_mix), w_in_bf[:, :SHIFT_COLS], w_in_bf[:, SHIFT_COLS:])

    w_wa = jnp.zeros((LORA_WA, 2 * C_A), F32)
    w_wa = w_wa.at[:D_DECAY_LORA, :C_A].set(decay_w2[lyr]).at[D_DECAY_LORA:, C_A:].set(iclr_a2[lyr])
    prep_w = (row(tshift_mu), row(decay_w0), row(iclr_a0), w_wa.astype(BF16), gate_g2[lyr].astype(BF16),
              row(k_k), row(k_a))
    st_p = _rwkv_prep_prompt(p_a, 0, n_p, t_p, prep_w)
    st_s = _rwkv_prep_sample(p_a, rows_p, n_s, t_s, state_shift[lyr], prep_w)

    scan_w = (row(lnx_w), row(lnx_b), row(r_k))
    sb_p = _pick_tile(n_p, 1, SCAN_UNROLL)
    ya_p, wkv_p = _wkv_scan(st_p, None, n_p, t_p, CHUNK, sb_p, sb_p, *scan_w)
    sb_s = _pick_tile(n_s, 1, 16)
    ya_s, wkv_s = _wkv_scan(st_s, state_wkv[lyr], n_s, t_s, t_s, sb_s, _pick_tile(sb_s, 1, SCAN_UNROLL), *scan_w)

    conv_w = (row(glu_b), dw_weight[lyr], row(dw_bias), row(conv_ln_w), row(conv_ln_b))
    yb_p, conv_p = _conv_prompt(p_b, 0, n_p, t_p, n_pad, conv_w)
    yb_s, conv_s = _conv_sample(p_b, rows_p, n_s, t_s, state_conv[lyr], conv_w)

    w_out_bf = w_out[lyr].astype(BF16)
    wr = jnp.zeros((D_MODEL, LANES), F32).at[:, :N_EXPERTS].set(w_router[lyr])
    wr_hi = wr.astype(BF16)
    wr_lo = (wr - wr_hi.astype(F32)).astype(BF16)
    b_r = jnp.zeros((1, LANES), F32).at[0, :N_EXPERTS].set(b_router[lyr])
    x1, h2, idx, gates, rank, counts = _out_route(ya_p, ya_s, yb_p, yb_s, x_all, w_out_bf[:C_A], w_out_bf[C_A:],
                                                  row(norm_ffn), wr_hi, wr_lo, b_r)

    asg, block_expert, n_used = _routing(idx, rank, counts)
    yt = _moe_ffn(asg, block_expert, n_used, h2, w_gate[lyr], b_gate[lyr], w_up[lyr], b_up[lyr],
                  w_down[lyr], b_down[lyr])
    y = _combine(yt, x1, gates, norm_final.reshape(1, -1))

    y_prompt = y[:rows_p].reshape(n_p, t_p, D_MODEL)[:, n_pad + N_META:]
    y_sample = y[rows_p:].reshape(n_s, t_s, D_MODEL)
    shift_p = p_a[t_p - 1:rows_p:t_p]
    shift_s = p_a[rows_p + t_s - 1::t_s]
    return (y_prompt.astype(x_prompt.dtype), y_sample.astype(x_sample.dtype),
            wkv_p[None], shift_p[None], conv_p[None], wkv_s[None], shift_s[None], conv_s[None])


def kernel(x_prompt, x_sample, state_wkv, state_shift, state_conv, meta_tokens, norm_mix, w_in, tshift_mu, decay_w0, decay_w2, iclr_a0, iclr_a2, gate_g2, k_k, k_a, r_k, lnx_w, lnx_b, glu_b, dw_weight, dw_bias, conv_ln_w, conv_ln_b, w_out, norm_ffn, w_router, b_router, w_gate, b_gate, w_up, b_up, w_down, b_down, norm_final):
    assert w_in.shape[0] == 1, "single trunk layer"
    return _forward(x_prompt, x_sample, state_wkv, state_shift, state_conv, meta_tokens, norm_mix, w_in,
                    tshift_mu, decay_w0, decay_w2, iclr_a0, iclr_a2, gate_g2, k_k, k_a, r_k, lnx_w, lnx_b,
                    glu_b, dw_weight, dw_bias, conv_ln_w, conv_ln_b, w_out, norm_ffn, w_router, b_router,
                    w_gate, b_gate, w_up, b_up, w_down, b_down, norm_final)
```

```python
import functools

import jax
import jax.numpy as jnp
from jax import lax
from jax.experimental import pallas as pl
from jax.experimental.pallas import tpu as pltpu

F32 = jnp.float32
BF16 = jnp.bfloat16

D_MODEL = 1024
N_META = 16
C_A = 512
HEAD = 64
N_HEADS = C_A // HEAD
C_B = 512
CONV_W = 31
D_DECAY_LORA = 64
D_AAA_LORA = 64
D_GATE_LORA = 128
LORA_WA = D_DECAY_LORA + D_AAA_LORA
SHIFT_COLS = 3 * C_A + LORA_WA + D_GATE_LORA
N_EXPERTS = 32
TOP_K = 4
SWIGLU_LIMIT = 7.0
SWIGLU_ALPHA = 1.702
RMS_EPS = 1e-5
LN_EPS = 1e-5
GN_EPS = 64e-5

LANES = 128
CHUNK = 64
SCAN_UNROLL = 2
CONV_HALO = 32
ROW_TILE = 512
MOE_ROWS = 256
VMEM_LIMIT = 56 * 1024 * 1024


def _cparams(sem):
    return pltpu.CompilerParams(dimension_semantics=sem, vmem_limit_bytes=VMEM_LIMIT)


def _dot(a, b):
    return jnp.dot(a.astype(BF16), b.astype(BF16), preferred_element_type=F32)


def _dot_nt(a, b):
    return lax.dot_general(a.astype(BF16), b.astype(BF16), (((1,), (1,)), ((), ())),
                           preferred_element_type=F32)


def _dot_tn(a, b):
    return lax.dot_general(a.astype(BF16), b.astype(BF16), (((0,), (0,)), ((), ())),
                           preferred_element_type=F32)


def _split2(x):
    hi = x.astype(BF16)
    lo = (x - hi.astype(F32)).astype(BF16)
    return hi, lo


def _split3(x):
    hi = x.astype(BF16)
    r1 = x - hi.astype(F32)
    mid = r1.astype(BF16)
    lo = (r1 - mid.astype(F32)).astype(BF16)
    return hi, mid, lo


def _dot_exact_rhs(x, m_bf16):
    hi, mid, lo = _split3(x)
    d = functools.partial(jnp.dot, preferred_element_type=F32)
    return d(hi, m_bf16) + d(mid, m_bf16) + d(lo, m_bf16)


def _dot_exact_lhs(m_bf16, x):
    hi, mid, lo = _split3(x)
    d = functools.partial(jnp.dot, preferred_element_type=F32)
    return d(m_bf16, hi) + d(m_bf16, mid) + d(m_bf16, lo)


def _rms(x, g):
    return x * lax.rsqrt(jnp.mean(x * x, axis=-1, keepdims=True) + RMS_EPS) * g


def _sigmoid(x):
    return 1.0 / (1.0 + jnp.exp(-x))


def _head_ones():
    i = lax.broadcasted_iota(jnp.int32, (C_A, C_A), 0) // HEAD
    j = lax.broadcasted_iota(jnp.int32, (C_A, C_A), 1) // HEAD
    return (i == j).astype(BF16)


def _pick_tile(n, mult, cap):
    best = mult
    t = mult
    while t <= min(n, cap):
        if n % t == 0:
            best = t
        t += mult
    assert n % best == 0, (n, mult)
    return best


def _in_proj_kernel(x_ref, g_ref, wa_ref, wb_ref, pa_ref, pb_ref):
    h = _rms(x_ref[...], g_ref[...]).astype(BF16)
    pa_ref[...] = jnp.dot(h, wa_ref[...], preferred_element_type=F32)
    pb_ref[...] = jnp.dot(h, wb_ref[...], preferred_element_type=F32)


def _in_proj(x, g, w_a, w_b):
    n = x.shape[0]
    tm = _pick_tile(n, 8, ROW_TILE)
    return pl.pallas_call(
        _in_proj_kernel,
        out_shape=(jax.ShapeDtypeStruct((n, SHIFT_COLS), F32), jax.ShapeDtypeStruct((n, 2 * C_B), F32)),
        grid=(n // tm,),
        in_specs=[pl.BlockSpec((tm, D_MODEL), lambda i: (i, 0)),
                  pl.BlockSpec((1, D_MODEL), lambda i: (0, 0)),
                  pl.BlockSpec((D_MODEL, SHIFT_COLS), lambda i: (0, 0)),
                  pl.BlockSpec((D_MODEL, 2 * C_B), lambda i: (0, 0))],
        out_specs=(pl.BlockSpec((tm, SHIFT_COLS), lambda i: (i, 0)),
                   pl.BlockSpec((tm, 2 * C_B), lambda i: (i, 0))),
        compiler_params=_cparams(("parallel",)),
        name="in_proj",
    )(x, g, w_a, w_b)


def _prep_math(p, pprev, mu, w0, a0, w_wa, w_g, k_k, k_a, ones_h, outs):
    r_ref, k_ref, v_ref, kk_ref, b_ref, lw_ref, g_ref = outs
    xs = p + (pprev - p) * mu
    r = xs[:, 0:C_A]
    k = xs[:, C_A:2 * C_A]
    v = xs[:, 2 * C_A:3 * C_A]
    lo = xs[:, 3 * C_A:3 * C_A + LORA_WA]
    g_lo = xs[:, 3 * C_A + LORA_WA:]
    lane = lax.broadcasted_iota(jnp.int32, lo.shape, 1)
    lo = jnp.where(lane < D_DECAY_LORA, jnp.tanh(lo), lo)
    wa = _dot(lo, w_wa)
    z = -(w0 + wa[:, :C_A])
    w = -(jnp.maximum(z, 0.0) + jnp.log(1.0 + jnp.exp(-jnp.abs(z)))) - 0.5
    lw_ref[...] = -jnp.exp(w)
    a = _sigmoid(a0 + wa[:, C_A:])
    g_ref[...] = _dot(_sigmoid(g_lo), w_g)
    kk = k * k_k
    hi, lo2 = _split2(kk * kk)
    ss = jnp.dot(hi, ones_h, preferred_element_type=F32) + jnp.dot(lo2, ones_h, preferred_element_type=F32)
    kk = kk / jnp.maximum(jnp.sqrt(ss), 1e-12)
    r_ref[...] = r
    v_ref[...] = v
    k_ref[...] = k * (1.0 + (a - 1.0) * k_a)
    kk_ref[...] = kk
    b_ref[...] = kk * a


def _prep_prompt_kernel(p_ref, halo_ref, mu_ref, w0_ref, a0_ref, wwa_ref, wg_ref, kk_ref_w, ka_ref, *outs):
    p = p_ref[...]
    prev_last = jnp.where(pl.program_id(1) == 0, 0.0, halo_ref[7:8, :])
    rolled = pltpu.roll(p, 1, 0)
    row = lax.broadcasted_iota(jnp.int32, p.shape, 0)
    pprev = jnp.where(row == 0, prev_last, rolled)
    _prep_math(p, pprev, mu_ref[...], w0_ref[...], a0_ref[...], wwa_ref[...], wg_ref[...],
               kk_ref_w[...], ka_ref[...], _head_ones(), outs)


def _prep_sample_kernel(p_ref, shift_ref, mu_ref, w0_ref, a0_ref, wwa_ref, wg_ref, kk_ref_w, ka_ref,
                        *outs_and_scratch):
    outs = outs_and_scratch
    p = p_ref[...]
    sb = shift_ref.shape[0]
    t_len = p.shape[0] // sb
    first = jnp.broadcast_to(shift_ref[...][:, None, :], (sb, t_len, SHIFT_COLS)).reshape(p.shape)
    row = lax.broadcasted_iota(jnp.int32, p.shape, 0)
    pprev = jnp.where(row % t_len == 0, first, pltpu.roll(p, 1, 0))
    _prep_math(p, pprev, mu_ref[...], w0_ref[...], a0_ref[...], wwa_ref[...], wg_ref[...],
               kk_ref_w[...], ka_ref[...], _head_ones(), outs)


def _prep_weight_specs(nd):
    z = (lambda *_: (0, 0))
    del nd
    return [pl.BlockSpec((1, SHIFT_COLS), z), pl.BlockSpec((1, C_A), z), pl.BlockSpec((1, C_A), z),
            pl.BlockSpec((LORA_WA, 2 * C_A), z), pl.BlockSpec((D_GATE_LORA, C_A), z),
            pl.BlockSpec((1, C_A), z), pl.BlockSpec((1, C_A), z)]


def _rwkv_prep_prompt(p_a, row0, n_seq, t_len, wts):
    tm = _pick_tile(t_len, 8, 704)
    nt = t_len // tm
    assert row0 % tm == 0
    base = row0 // tm
    out = jax.ShapeDtypeStruct((n_seq * t_len, C_A), F32)
    ospec = pl.BlockSpec((tm, C_A), lambda b, c: (b * nt + c, 0))
    return pl.pallas_call(
        _prep_prompt_kernel,
        out_shape=(out,) * 7,
        grid=(n_seq, nt),
        in_specs=[pl.BlockSpec((tm, SHIFT_COLS), lambda b, c: (base + b * nt + c, 0)),
                  pl.BlockSpec((8, SHIFT_COLS),
                               lambda b, c: (jnp.maximum((base + b * nt + c) * (tm // 8) - 1, 0), 0)),
                  ] + _prep_weight_specs(2),
        out_specs=(ospec,) * 7,
        compiler_params=_cparams(("parallel", "parallel")),
        name="rwkv_prep_prompt",
    )(p_a, p_a, *wts)


def _rwkv_prep_sample(p_a, row0, n_seq, t_len, shift, wts):
    sb = _pick_tile(n_seq, 8, 64)
    tm = sb * t_len
    assert row0 % tm == 0 and t_len % 8 == 0
    base = row0 // tm
    out = jax.ShapeDtypeStruct((n_seq * t_len, C_A), F32)
    ospec = pl.BlockSpec((tm, C_A), lambda i: (i, 0))
    return pl.pallas_call(
        _prep_sample_kernel,
        out_shape=(out,) * 7,
        grid=(n_seq // sb,),
        in_specs=[pl.BlockSpec((tm, SHIFT_COLS), lambda i: (base + i, 0)),
                  pl.BlockSpec((sb, SHIFT_COLS), lambda i: (i, 0)),
                  ] + _prep_weight_specs(1),
        out_specs=(ospec,) * 7,
        compiler_params=_cparams(("parallel",)),
        name="rwkv_prep_sample",
    )(p_a, shift, *wts)


def _scan_kernel(has_s0, unroll, *refs):
    if has_s0:
        (r_ref, k_ref, v_ref, kk_ref, b_ref, lw_ref, g_ref, s0_ref,
         lnw_ref, lnb_ref, rk_ref, y_ref, s_ref) = refs
    else:
        (r_ref, k_ref, v_ref, kk_ref, b_ref, lw_ref, g_ref,
         lnw_ref, lnb_ref, rk_ref, y_ref, s_ref) = refs
        s0_ref = None
    n_seq, c_len, _ = r_ref.shape

    @pl.when(pl.program_id(1) == 0)
    def _():
        if has_s0:
            s_ref[...] = s0_ref[...]
        else:
            s_ref[...] = jnp.zeros_like(s_ref)

    ti = lax.broadcasted_iota(jnp.int32, (c_len, c_len), 0)
    si = lax.broadcasted_iota(jnp.int32, (c_len, c_len), 1)
    tri_incl = (si <= ti).astype(BF16)
    strict = (si < ti).astype(F32)
    incl = (si <= ti).astype(F32)
    incl_signed = jnp.concatenate([incl, -incl], axis=1)
    ones_h = _head_ones()
    lnw, lnb, rk = lnw_ref[...], lnb_ref[...], rk_ref[...]
    heads = range(N_HEADS)
    hsl = [slice(h * HEAD, (h + 1) * HEAD) for h in heads]

    def gsum(x):
        hi, lo = _split2(x)
        return jnp.dot(hi, ones_h, preferred_element_type=F32) + jnp.dot(lo, ones_h, preferred_element_type=F32)

    def load(s):
        return ([ref[s] for ref in (r_ref, k_ref, v_ref, kk_ref, b_ref, lw_ref, g_ref)],
                [s_ref[s, h] for h in heads])

    def compute(streams, states):
        r, k, v, kk, b, lw, g = streams
        cum = _dot_exact_lhs(tri_incl, lw)
        tot = cum[c_len - 1:c_len, :]
        g_inv = jnp.exp(-cum)
        g_end = jnp.exp(tot - cum)
        l_mat = jnp.concatenate([kk * jnp.exp(cum - lw), r * jnp.exp(cum)], axis=0).astype(BF16)
        kbh = jnp.concatenate([k * g_inv, b * g_inv], axis=0).astype(BF16)
        e_mat = jnp.concatenate([k * g_end, -(b * g_end)], axis=0).astype(BF16)
        g_tot = jnp.exp(tot)
        l_h = [l_mat[:, hs] for hs in hsl]
        v_h = [v[:, hs] for hs in hsl]
        n_ub = [_dot_nt(l_h[h][:c_len], kbh[c_len:, hsl[h]]) * strict for h in heads]
        a_vk = [_dot_nt(l_h[h][:c_len], kbh[:c_len, hsl[h]]) * strict for h in heads]
        pm = [_dot_nt(l_h[h], states[h]) for h in heads]
        x = [pm[h][:c_len] + _dot(a_vk[h], v_h[h]) for h in heads]
        x = [x[h] - _dot(n_ub[h], x[h]) for h in heads]
        pw = n_ub
        m = 2
        while m < c_len:
            pw = [_dot(p, p) for p in pw]
            x = [x[h] + _dot(pw[h], x[h]) for h in heads]
            m *= 2
        b_m = [_dot_nt(l_h[h][c_len:], kbh[:, hsl[h]]) * incl_signed for h in heads]
        vu = [jnp.concatenate([v_h[h], x[h]], axis=0) for h in heads]
        y = jnp.concatenate([pm[h][c_len:] + _dot(b_m[h], vu[h]) for h in heads], axis=1)
        new_states = [states[h] * g_tot[:, hsl[h]] + _dot_tn(vu[h], e_mat[:, hsl[h]]) for h in heads]
        mu = gsum(y) * (1.0 / HEAD)
        d = y - mu
        var = gsum(d * d) * (1.0 / HEAD)
        yn = d * lax.rsqrt(var + GN_EPS) * lnw + lnb
        bonus = gsum(r * k * rk) * v
        return (yn + bonus) * g, new_states

    def group(base):
        idx = [base + i for i in range(unroll)]
        loaded = [load(s) for s in idx]
        results = [compute(*d) for d in loaded]
        for s, (y, new_states) in zip(idx, results):
            y_ref[s] = y
            for h in heads:
                s_ref[s, h] = new_states[h]

    if n_seq == unroll:
        group(0)
    else:
        def body(i, carry):
            group(i * unroll)
            return carry
        lax.fori_loop(0, n_seq // unroll, body, 0)


def _wkv_scan(streams, s0, n_seq, t_len, c_len, sb, unroll, lnw, lnb, rk):
    nc = t_len // c_len
    streams = [a.reshape(n_seq, t_len, C_A) for a in streams]
    dspec = pl.BlockSpec((sb, c_len, C_A), lambda i, c: (i, c, 0))
    sspec = pl.BlockSpec((sb, N_HEADS, HEAD, HEAD), lambda i, c: (i, 0, 0, 0))
    wspec = pl.BlockSpec((1, C_A), lambda i, c: (0, 0))
    ins = list(streams) + ([s0] if s0 is not None else []) + [lnw, lnb, rk]
    in_specs = [dspec] * 7 + ([sspec] if s0 is not None else []) + [wspec] * 3
    y, s_new = pl.pallas_call(
        functools.partial(_scan_kernel, s0 is not None, unroll),
        out_shape=(jax.ShapeDtypeStruct((n_seq, t_len, C_A), F32),
                   jax.ShapeDtypeStruct((n_seq, N_HEADS, HEAD, HEAD), F32)),
        grid=(n_seq // sb, nc),
        in_specs=in_specs,
        out_specs=(dspec, sspec),
        compiler_params=_cparams(("parallel", "arbitrary")),
        name="wkv_scan_c%d" % c_len,
    )(*ins)
    return y.reshape(n_seq * t_len, C_A), s_new


def _conv_tail(z, dwb, lnw, lnb):
    z = z + dwb
    mu = jnp.mean(z, axis=-1, keepdims=True)
    d = z - mu
    var = jnp.mean(d * d, axis=-1, keepdims=True)
    z = d * lax.rsqrt(var + LN_EPS) * lnw + lnb
    return z * _sigmoid(z)


def _glu(pb, glu_b):
    u = pb + glu_b
    return u[:, :C_B] * _sigmoid(u[:, C_B:])


def _conv_prompt_kernel(n_pad, pb_ref, halo_ref, glub_ref, dw_ref, dwb_ref, lnw_ref, lnb_ref,
                        y_ref, st_ref, ext_ref):
    c = pl.program_id(1)
    tm = pb_ref.shape[0]
    glub = glub_ref[...]
    halo = jnp.where(c == 0, 0.0, _glu(halo_ref[...], glub))
    u = _glu(pb_ref[...], glub)
    t_glob = c * tm + lax.broadcasted_iota(jnp.int32, u.shape, 0)
    u = jnp.where(t_glob < n_pad, 0.0, u)
    ext_ref[0:CONV_HALO, :] = halo
    ext_ref[CONV_HALO:, :] = u
    off = CONV_HALO - (CONV_W - 1)
    rb = 32

    def blk(i, carry):
        base = pl.multiple_of(i * rb, rb)
        win = ext_ref[pl.ds(base, rb + CONV_HALO), :]
        acc = jnp.zeros((rb, C_B), F32)
        for s in range(8):
            taps = [j for j in range(CONV_W) if (off + j) % 8 == s]
            shifted = win if s == 0 else pltpu.roll(win, win.shape[0] - s, 0)
            for j in taps:
                a = off + j - s
                acc = acc + shifted[a:a + rb] * dw_ref[j:j + 1, :]
        y_ref[pl.ds(base, rb), :] = _conv_tail(acc, dwb_ref[...], lnw_ref[...], lnb_ref[...])
        return carry

    lax.fori_loop(0, tm // rb, blk, 0)

    @pl.when(c == pl.num_programs(1) - 1)
    def _():
        st_ref[0] = ext_ref[CONV_HALO + tm - (CONV_W - 1):CONV_HALO + tm, :]


def _conv_prompt(p_b, row0, n_seq, t_len, n_pad, wts):
    tm = _pick_tile(t_len, CONV_HALO, 704)
    nt = t_len // tm
    assert row0 % tm == 0
    base = row0 // tm
    z2 = lambda b, c: (0, 0)
    return pl.pallas_call(
        functools.partial(_conv_prompt_kernel, n_pad),
        out_shape=(jax.ShapeDtypeStruct((n_seq * t_len, C_B), F32),
                   jax.ShapeDtypeStruct((n_seq, CONV_W - 1, C_B), F32)),
        grid=(n_seq, nt),
        in_specs=[pl.BlockSpec((tm, 2 * C_B), lambda b, c: (base + b * nt + c, 0)),
                  pl.BlockSpec((CONV_HALO, 2 * C_B),
                               lambda b, c: (jnp.maximum((base + b * nt + c) * (tm // CONV_HALO) - 1, 0), 0)),
                  pl.BlockSpec((1, 2 * C_B), z2), pl.BlockSpec((CONV_W, C_B), z2),
                  pl.BlockSpec((1, C_B), z2), pl.BlockSpec((1, C_B), z2), pl.BlockSpec((1, C_B), z2)],
        out_specs=(pl.BlockSpec((tm, C_B), lambda b, c: (b * nt + c, 0)),
                   pl.BlockSpec((1, CONV_W - 1, C_B), lambda b, c: (b, 0, 0))),
        scratch_shapes=[pltpu.VMEM((CONV_HALO + tm, C_B), F32)],
        compiler_params=_cparams(("parallel", "arbitrary")),
        name="conv_prompt",
    )(p_b, p_b, *wts)


def _conv_sample_kernel(pb_ref, buf_ref, glub_ref, dw_ref, dwb_ref, lnw_ref, lnb_ref,
                        y_ref, st_ref, ext_ref):
    sb, t_len = buf_ref.shape[0], pb_ref.shape[0] // buf_ref.shape[0]
    hist = CONV_W - 1
    u = _glu(pb_ref[...], glub_ref[...])
    lead = ext_ref.shape[1] - hist - t_len
    ext_ref[:, lead:lead + hist, :] = buf_ref[...]
    ext_ref[:, lead + hist:, :] = u.reshape(sb, t_len, C_B)
    acc = jnp.zeros((sb, t_len, C_B), F32)
    for j in range(CONV_W):
        acc = acc + ext_ref[:, lead + j:lead + j + t_len, :] * dw_ref[j:j + 1, :]
    z = _conv_tail(acc.reshape(sb * t_len, C_B), dwb_ref[...], lnw_ref[...], lnb_ref[...])
    y_ref[...] = z
    st_ref[...] = ext_ref[:, lead + t_len:, :]


def _conv_sample(p_b, row0, n_seq, t_len, buf, wts):
    sb = _pick_tile(n_seq, 8, 32)
    tm = sb * t_len
    assert row0 % tm == 0 and t_len % 8 == 0
    base = row0 // tm
    hist = CONV_W - 1
    ext_rows = -(-(hist + t_len) // 8) * 8
    z1 = lambda i: (0, 0)
    return pl.pallas_call(
        _conv_sample_kernel,
        out_shape=(jax.ShapeDtypeStruct((n_seq * t_len, C_B), F32),
                   jax.ShapeDtypeStruct((n_seq, hist, C_B), F32)),
        grid=(n_seq // sb,),
        in_specs=[pl.BlockSpec((tm, 2 * C_B), lambda i: (base + i, 0)),
                  pl.BlockSpec((sb, hist, C_B), lambda i: (i, 0, 0)),
                  pl.BlockSpec((1, 2 * C_B), z1), pl.BlockSpec((CONV_W, C_B), z1),
                  pl.BlockSpec((1, C_B), z1), pl.BlockSpec((1, C_B), z1), pl.BlockSpec((1, C_B), z1)],
        out_specs=(pl.BlockSpec((tm, C_B), lambda i: (i, 0)),
                   pl.BlockSpec((sb, hist, C_B), lambda i: (i, 0, 0))),
        scratch_shapes=[pltpu.VMEM((sb, ext_rows, C_B), F32)],
        compiler_params=_cparams(("parallel",)),
        name="conv_sample",
    )(p_b, buf, *wts)


TILE_SUB = D_MODEL // LANES


def _store_token_tiles(ref, x, lead=()):
    rows = x.shape[0]
    for c in range(TILE_SUB):
        ref[lead + (pl.ds(c, rows, stride=TILE_SUB), slice(None))] = x[:, c * LANES:(c + 1) * LANES]


def _load_token_tiles(ref, rows, lead=()):
    return jnp.concatenate([ref[lead + (pl.ds(c, rows, stride=TILE_SUB), slice(None))]
                            for c in range(TILE_SUB)], axis=1)


def _out_route_kernel(n_first, yaf_ref, yas_ref, ybf_ref, ybs_ref, x_ref, woa_ref, wob_ref, g_ref,
                      wrh_ref, wrl_ref, br_ref, x1_ref, h2_ref, idx_ref, gate_ref, rank_ref, cnt_ref, run_ref):
    i = pl.program_id(0)
    first = i < n_first
    ya = jnp.where(first, yaf_ref[...], yas_ref[...])
    yb = jnp.where(first, ybf_ref[...], ybs_ref[...])
    x1 = x_ref[...] + _dot(ya, woa_ref[...]) + _dot(yb, wob_ref[...])
    x1_ref[...] = x1
    h2 = _rms(x1, g_ref[...])
    _store_token_tiles(h2_ref, h2)
    hi, lo = _split2(h2)
    d = functools.partial(jnp.dot, preferred_element_type=F32)
    logits = d(hi, wrh_ref[...]) + d(hi, wrl_ref[...]) + d(lo, wrh_ref[...]) + br_ref[...]
    lane = lax.broadcasted_iota(jnp.int32, logits.shape, 1)
    logits = jnp.where(lane < N_EXPERTS, logits, -jnp.inf)
    idx_out = jnp.zeros(logits.shape, jnp.int32)
    val_out = jnp.full(logits.shape, -jnp.inf, F32)
    onehots = []
    for kk in range(TOP_K):
        m = jnp.max(logits, axis=-1, keepdims=True)
        sel = jnp.min(jnp.where(logits == m, lane, LANES), axis=-1, keepdims=True)
        idx_out = jnp.where(lane == kk, sel, idx_out)
        val_out = jnp.where(lane == kk, m, val_out)
        hit = lane == sel
        onehots.append(hit.astype(F32))
        logits = jnp.where(hit, -jnp.inf, logits)
    e = jnp.exp(val_out - jnp.max(val_out, axis=-1, keepdims=True))
    idx_ref[...] = idx_out
    gate_ref[...] = e / jnp.sum(e, axis=-1, keepdims=True)

    @pl.when(i == 0)
    def _():
        run_ref[...] = jnp.zeros_like(run_ref)

    tm = logits.shape[0]
    total = onehots[0] + onehots[1] + onehots[2] + onehots[3]
    ti = lax.broadcasted_iota(jnp.int32, (tm, tm), 0)
    si = lax.broadcasted_iota(jnp.int32, (tm, tm), 1)
    before = jnp.dot((si < ti).astype(BF16), total.astype(BF16), preferred_element_type=F32) + run_ref[...]
    rank = jnp.zeros(logits.shape, F32)
    for kk in range(TOP_K):
        rank = jnp.where(lane == kk, jnp.sum(onehots[kk] * before, axis=-1, keepdims=True), rank)
    rank_ref[...] = rank.astype(jnp.int32)
    run_ref[...] += jnp.sum(total, axis=0, keepdims=True)
    cnt_ref[...] = run_ref[...]


def _out_route(ya_f, ya_s, yb_f, yb_s, x, wo_a, wo_b, g, wr_hi, wr_lo, b_r):
    n = x.shape[0]
    n_f = ya_f.shape[0]
    tm = ROW_TILE
    while n_f % tm or (n - n_f) % tm:
        tm //= 2
    assert tm % 8 == 0
    nf_t, ns_t = n_f // tm, (n - n_f) // tm
    z = lambda i: (0, 0)
    row = lambda w: pl.BlockSpec((tm, w), lambda i: (i, 0))
    fst = lambda w: pl.BlockSpec((tm, w), lambda i: (jnp.minimum(i, nf_t - 1), 0))
    snd = lambda w: pl.BlockSpec((tm, w), lambda i: (jnp.maximum(i - nf_t, 0), 0))
    return pl.pallas_call(
        functools.partial(_out_route_kernel, nf_t),
        out_shape=(jax.ShapeDtypeStruct((n, D_MODEL), F32), jax.ShapeDtypeStruct((n * TILE_SUB, LANES), F32),
                   jax.ShapeDtypeStruct((n, LANES), jnp.int32), jax.ShapeDtypeStruct((n, LANES), F32),
                   jax.ShapeDtypeStruct((n, LANES), jnp.int32), jax.ShapeDtypeStruct((1, LANES), F32)),
        grid=(nf_t + ns_t,),
        in_specs=[fst(C_A), snd(C_A), fst(C_B), snd(C_B), row(D_MODEL),
                  pl.BlockSpec((C_A, D_MODEL), z), pl.BlockSpec((C_B, D_MODEL), z),
                  pl.BlockSpec((1, D_MODEL), z),
                  pl.BlockSpec((D_MODEL, LANES), z), pl.BlockSpec((D_MODEL, LANES), z),
                  pl.BlockSpec((1, LANES), z)],
        out_specs=(row(D_MODEL), pl.BlockSpec((tm * TILE_SUB, LANES), lambda i: (i, 0)), row(LANES), row(LANES),
                   row(LANES), pl.BlockSpec((1, LANES), z)),
        scratch_shapes=[pltpu.VMEM((1, LANES), F32)],
        compiler_params=_cparams(("arbitrary",)),
        name="out_route",
    )(ya_f, ya_s, yb_f, yb_s, x, wo_a, wo_b, g, wr_hi, wr_lo, b_r)


FFN_BUFS = 3


def _ffn_kernel(n_tok, asg_ref, bexp_ref, nblk_ref, h2_hbm, wg_ref, bg_ref, wu_ref, bu_ref, wd_ref, bd_ref,
                out_hbm, *scratch):
    xbufs, ybufs = scratch[:FFN_BUFS], scratch[FFN_BUFS:2 * FFN_BUFS]
    wg_bf, wu_bf, wd_bf, gsem, ssem = scratch[2 * FFN_BUFS:]
    j = pl.program_id(0)
    last = pl.num_programs(0) - 1
    n_used = nblk_ref[0]
    rows = MOE_ROWS

    def gather(i, tab_row, s):
        tok = jnp.minimum(lax.shift_right_logical(asg_ref[tab_row, i], jnp.int32(2)), n_tok - 1)
        return pltpu.make_async_copy(h2_hbm.at[pl.ds(pl.multiple_of(tok * TILE_SUB, TILE_SUB), TILE_SUB), :],
                                     xbufs[s].at[pl.ds(i * TILE_SUB, TILE_SUB), :], gsem.at[s])

    def scatter(i, tab_row, s):
        dst = asg_ref[tab_row, i]
        return pltpu.make_async_copy(ybufs[s].at[pl.ds(i * TILE_SUB, TILE_SUB), :],
                                     out_hbm.at[pl.ds(pl.multiple_of(dst * TILE_SUB, TILE_SUB), TILE_SUB), :],
                                     ssem.at[s])

    def start_all(fn, unrolled):
        if unrolled:
            for i in range(rows):
                fn(i).start(priority=i % 2)
        else:
            def body(i, carry):
                fn(i).start()
                return carry
            lax.fori_loop(0, rows, body, 0)

    def wait_gather(s):
        pltpu.make_async_copy(h2_hbm.at[pl.ds(0, rows * TILE_SUB), :], xbufs[s], gsem.at[s]).wait()

    def wait_scatter(s):
        pltpu.make_async_copy(ybufs[s], out_hbm.at[pl.ds(0, rows * TILE_SUB), :], ssem.at[s]).wait()

    @pl.when(j == 0)
    def _():
        for yb in ybufs:
            yb[...] = jnp.zeros_like(yb)
        start_all(lambda i: gather(i, 1, 0), False)
        start_all(lambda i: gather(i, 2, 1), False)

    def step(slot):
        prev = (slot + FFN_BUFS - 1) % FFN_BUFS
        nxt = (slot + 1) % FFN_BUFS

        def start_next(unrolled):
            start_all(lambda i: gather(i, j + 3, prev), unrolled)
            start_all(lambda i: scatter(i, j, prev), unrolled)

        @pl.when(j >= 2)
        def _():
            wait_scatter(slot)

        wait_gather(slot)

        @pl.when(j < n_used)
        def _():
            new_expert = jnp.logical_or(j == 0, bexp_ref[j] != bexp_ref[jnp.maximum(j - 1, 0)])

            @pl.when(new_expert)
            def _():
                wg_bf[...] = wg_ref[0].astype(BF16)
                wu_bf[...] = wu_ref[0].astype(BF16)
                wd_bf[...] = wd_ref[0].astype(BF16)

            x = _load_token_tiles(xbufs[slot], rows).astype(BF16)
            start_next(True)
            gt = jnp.minimum(jnp.dot(x, wg_bf[...], preferred_element_type=F32) + bg_ref[0], SWIGLU_LIMIT)
            up = jnp.clip(jnp.dot(x, wu_bf[...], preferred_element_type=F32) + bu_ref[0],
                          -SWIGLU_LIMIT, SWIGLU_LIMIT)
            act = (up + 1.0) * gt * _sigmoid(SWIGLU_ALPHA * gt)
            y = jnp.dot(act.astype(BF16), wd_bf[...], preferred_element_type=F32) + bd_ref[0]
            _store_token_tiles(ybufs[slot], y)

        @pl.when(j >= n_used)
        def _():
            start_next(False)

        @pl.when(j == last)
        def _():
            for s in (nxt, prev):
                wait_gather(s)
                wait_scatter(s)

    for s in range(FFN_BUFS):
        pl.when(lax.rem(j, FFN_BUFS) == s)(functools.partial(step, s))


def _moe_ffn(asg, block_expert, n_used, h2t, w_gate, b_gate, w_up, b_up, w_down, b_down):
    n_steps = block_expert.shape[0]
    n_tok = h2t.shape[0] // TILE_SUB
    d_ff = w_gate.shape[2]
    out_rows = n_steps * MOE_ROWS
    wspec = lambda shape: pl.BlockSpec((1,) + shape, lambda j, a, be, nb: (be[j], 0, 0))
    tiles = (MOE_ROWS * TILE_SUB, LANES)
    return pl.pallas_call(
        functools.partial(_ffn_kernel, n_tok),
        out_shape=jax.ShapeDtypeStruct((out_rows * TILE_SUB, LANES), F32),
        grid_spec=pltpu.PrefetchScalarGridSpec(
            num_scalar_prefetch=3,
            grid=(n_steps,),
            in_specs=[pl.BlockSpec(memory_space=pl.ANY),
                      wspec((D_MODEL, d_ff)), wspec((1, d_ff)),
                      wspec((D_MODEL, d_ff)), wspec((1, d_ff)),
                      wspec((d_ff, D_MODEL)), wspec((1, D_MODEL))],
            out_specs=pl.BlockSpec(memory_space=pl.ANY),
            scratch_shapes=[pltpu.VMEM(tiles, F32)] * (2 * FFN_BUFS) + [
                pltpu.VMEM((D_MODEL, d_ff), BF16), pltpu.VMEM((D_MODEL, d_ff), BF16),
                pltpu.VMEM((d_ff, D_MODEL), BF16),
                pltpu.SemaphoreType.DMA((FFN_BUFS,)), pltpu.SemaphoreType.DMA((FFN_BUFS,))]),
        compiler_params=_cparams(("arbitrary",)),
        name="moe_ffn",
    )(asg, block_expert, n_used, h2t, w_gate, b_gate.reshape(N_EXPERTS, 1, d_ff),
      w_up, b_up.reshape(N_EXPERTS, 1, d_ff), w_down, b_down.reshape(N_EXPERTS, 1, D_MODEL))


def _combine_kernel(e_ref, x1_ref, gate_ref, g_ref, y_ref):
    tm = x1_ref.shape[0]
    gates = gate_ref[...]
    acc = x1_ref[...]
    for kk in range(TOP_K):
        yk = jnp.concatenate([e_ref[pl.ds(kk * TILE_SUB + c, tm, stride=TOP_K * TILE_SUB), :]
                              for c in range(TILE_SUB)], axis=1)
        acc = acc + yk * gates[:, kk:kk + 1]
    y_ref[...] = _rms(acc, g_ref[...])


def _combine(yt, x1, gates, g_final):
    n = x1.shape[0]
    tm = _pick_tile(n, 8, 256)
    return pl.pallas_call(
        _combine_kernel,
        out_shape=jax.ShapeDtypeStruct((n, D_MODEL), F32),
        grid=(n // tm,),
        in_specs=[pl.BlockSpec((tm * TOP_K * TILE_SUB, LANES), lambda i: (i, 0)),
                  pl.BlockSpec((tm, D_MODEL), lambda i: (i, 0)),
                  pl.BlockSpec((tm, LANES), lambda i: (i, 0)),
                  pl.BlockSpec((1, D_MODEL), lambda i: (0, 0))],
        out_specs=pl.BlockSpec((tm, D_MODEL), lambda i: (i, 0)),
        compiler_params=_cparams(("parallel",)),
        name="moe_combine",
    )(yt, x1, gates, g_final)


def _routing(idx, rank, counts):
    n = idx.shape[0]
    n_assign = n * TOP_K
    flat_e = idx[:, :TOP_K].reshape(-1)
    rank = rank[:, :TOP_K].reshape(-1)
    counts = counts[0, :N_EXPERTS].astype(jnp.int32)
    padded = (counts + MOE_ROWS - 1) // MOE_ROWS * MOE_ROWS
    pad_end = jnp.cumsum(padded)
    pad_start = pad_end - padded
    n_blocks = (n_assign + N_EXPERTS * (MOE_ROWS - 1) + MOE_ROWS - 1) // MOE_ROWS
    dest = (pad_start[flat_e] + rank).astype(jnp.int32) + MOE_ROWS
    tab_rows = n_blocks + 1 + FFN_BUFS
    q = jnp.arange(tab_rows * MOE_ROWS, dtype=jnp.int32) - MOE_ROWS
    e_blk = jnp.sum((q[::MOE_ROWS, None] >= pad_end[None, :]).astype(jnp.int32), axis=1)
    e_of = jnp.broadcast_to(e_blk[:, None], (tab_rows, MOE_ROWS)).reshape(-1)
    waste = padded - counts
    waste_before = jnp.concatenate([jnp.zeros((1,), waste.dtype), jnp.cumsum(waste)])
    e_c = jnp.minimum(e_of, N_EXPERTS - 1)
    in_expert = MOE_ROWS + waste_before[e_c] + (q - pad_start[e_c] - counts[e_c])
    after = MOE_ROWS + waste_before[N_EXPERTS] + (q - pad_end[N_EXPERTS - 1])
    dump = jnp.where(q < 0, q + MOE_ROWS, jnp.where(e_of < N_EXPERTS, in_expert, after))
    asg = (n_assign + dump).astype(jnp.int32).at[dest].set(
        jnp.arange(n_assign, dtype=jnp.int32), unique_indices=True, mode='promise_in_bounds')
    n_used = (pad_end[-1] // MOE_ROWS).astype(jnp.int32)
    blk_start = jnp.arange(n_blocks + 1, dtype=jnp.int32) * MOE_ROWS
    block_expert = jnp.minimum(jnp.sum((blk_start[:, None] >= pad_end[None, :]).astype(jnp.int32), axis=1),
                               N_EXPERTS - 1)
    last_used = block_expert[jnp.maximum(n_used - 1, 0)]
    block_expert = jnp.where(jnp.arange(n_blocks + 1) < n_used, block_expert, last_used).astype(jnp.int32)
    return asg.reshape(tab_rows, MOE_ROWS), block_expert, n_used.reshape(1)


def _forward(x_prompt, x_sample, state_wkv, state_shift, state_conv, meta_tokens, norm_mix, w_in,
             tshift_mu, decay_w0, decay_w2, iclr_a0, iclr_a2, gate_g2, k_k, k_a, r_k, lnx_w, lnx_b,
             glu_b, dw_weight, dw_bias, conv_ln_w, conv_ln_b, w_out, norm_ffn, w_router, b_router,
             w_gate, b_gate, w_up, b_up, w_down, b_down, norm_final):
    n_p, seq, _ = x_prompt.shape
    n_s, t_s, _ = x_sample.shape
    t_real = N_META + seq
    n_pad = (-t_real) % CHUNK
    t_p = t_real + n_pad
    rows_p = n_p * t_p
    rows_s = n_s * t_s
    lyr = 0

    lead = jnp.concatenate([jnp.zeros((n_pad, D_MODEL), F32), meta_tokens.astype(F32)], axis=0)
    pieces = []
    for b in range(n_p):
        pieces += [lead, x_prompt[b]]
    x_all = jnp.concatenate(pieces + [x_sample.reshape(rows_s, D_MODEL)], axis=0)

    row = lambda a: a[lyr].reshape(1, -1).astype(F32)
    w_in_bf = w_in[lyr].astype(BF16)
    p_a, p_b = _in_proj(x_all, row(norm_mix), w_in_bf[:, :SHIFT_COLS], w_in_bf[:, SHIFT_COLS:])

    w_wa = jnp.zeros((LORA_WA, 2 * C_A), F32)
    w_wa = w_wa.at[:D_DECAY_LORA, :C_A].set(decay_w2[lyr]).at[D_DECAY_LORA:, C_A:].set(iclr_a2[lyr])
    prep_w = (row(tshift_mu), row(decay_w0), row(iclr_a0), w_wa.astype(BF16), gate_g2[lyr].astype(BF16),
              row(k_k), row(k_a))
    st_p = _rwkv_prep_prompt(p_a, 0, n_p, t_p, prep_w)
    st_s = _rwkv_prep_sample(p_a, rows_p, n_s, t_s, state_shift[lyr], prep_w)

    scan_w = (row(lnx_w), row(lnx_b), row(r_k))
    sb_p = _pick_tile(n_p, 1, SCAN_UNROLL)
    ya_p, wkv_p = _wkv_scan(st_p, None, n_p, t_p, CHUNK, sb_p, sb_p, *scan_w)
    sb_s = _pick_tile(n_s, 1, 16)
    ya_s, wkv_s = _wkv_scan(st_s, state_wkv[lyr], n_s, t_s, t_s, sb_s, _pick_tile(sb_s, 1, SCAN_UNROLL), *scan_w)

    conv_w = (row(glu_b), dw_weight[lyr], row(dw_bias), row(conv_ln_w), row(conv_ln_b))
    yb_p, conv_p = _conv_prompt(p_b, 0, n_p, t_p, n_pad, conv_w)
    yb_s, conv_s = _conv_sample(p_b, rows_p, n_s, t_s, state_conv[lyr], conv_w)

    w_out_bf = w_out[lyr].astype(BF16)
    wr = jnp.zeros((D_MODEL, LANES), F32).at[:, :N_EXPERTS].set(w_router[lyr])
    wr_hi = wr.astype(BF16)
    wr_lo = (wr - wr_hi.astype(F32)).astype(BF16)
    b_r = jnp.zeros((1, LANES), F32).at[0, :N_EXPERTS].set(b_router[lyr])
    x1, h2, idx, gates, rank, counts = _out_route(ya_p, ya_s, yb_p, yb_s, x_all, w_out_bf[:C_A], w_out_bf[C_A:],
                                                  row(norm_ffn), wr_hi, wr_lo, b_r)

    asg, block_expert, n_used = _routing(idx, rank, counts)
    yt = _moe_ffn(asg, block_expert, n_used, h2, w_gate[lyr], b_gate[lyr], w_up[lyr], b_up[lyr],
                  w_down[lyr], b_down[lyr])
    y = _combine(yt, x1, gates, norm_final.reshape(1, -1))

    y_prompt = y[:rows_p].reshape(n_p, t_p, D_MODEL)[:, n_pad + N_META:]
    y_sample = y[rows_p:].reshape(n_s, t_s, D_MODEL)
    shift_p = p_a[t_p - 1:rows_p:t_p]
    shift_s = p_a[rows_p + t_s - 1::t_s]
    return (y_prompt.astype(x_prompt.dtype), y_sample.astype(x_sample.dtype),
            wkv_p[None], shift_p[None], conv_p[None], wkv_s[None], shift_s[None], conv_s[None])


def kernel(x_prompt, x_sample, state_wkv, state_shift, state_conv, meta_tokens, norm_mix, w_in, tshift_mu, decay_w0, decay_w2, iclr_a0, iclr_a2, gate_g2, k_k, k_a, r_k, lnx_w, lnx_b, glu_b, dw_weight, dw_bias, conv_ln_w, conv_ln_b, w_out, norm_ffn, w_router, b_router, w_gate, b_gate, w_up, b_up, w_down, b_down, norm_final):
    assert w_in.shape[0] == 1, "single trunk layer"
    return _forward(x_prompt, x_sample, state_wkv, state_shift, state_conv, meta_tokens, norm_mix, w_in,
                    tshift_mu, decay_w0, decay_w2, iclr_a0, iclr_a2, gate_g2, k_k, k_a, r_k, lnx_w, lnx_b,
                    glu_b, dw_weight, dw_bias, conv_ln_w, conv_ln_b, w_out, norm_ffn, w_router, b_router,
                    w_gate, b_gate, w_up, b_up, w_down, b_down, norm_final)
```

```python
import functools

import jax
import jax.numpy as jnp
from jax import lax
from jax.experimental import pallas as pl
from jax.experimental.pallas import tpu as pltpu

F32 = jnp.float32
BF16 = jnp.bfloat16

D_MODEL = 1024
N_META = 16
C_A = 512
HEAD = 64
N_HEADS = C_A // HEAD
C_B = 512
CONV_W = 31
D_DECAY_LORA = 64
D_AAA_LORA = 64
D_GATE_LORA = 128
LORA_WA = D_DECAY_LORA + D_AAA_LORA
SHIFT_COLS = 3 * C_A + LORA_WA + D_GATE_LORA
N_EXPERTS = 32
EXPERT_BITS = 5
TOP_K = 4
SWIGLU_LIMIT = 7.0
SWIGLU_ALPHA = 1.702
RMS_EPS = 1e-5
LN_EPS = 1e-5
GN_EPS = 64e-5

LANES = 128
CHUNK = 64
SCAN_UNROLL = 2
CONV_HALO = 32
ROW_TILE = 512
MOE_ROWS = 256
VMEM_LIMIT = 56 * 1024 * 1024


def _cparams(sem):
    return pltpu.CompilerParams(dimension_semantics=sem, vmem_limit_bytes=VMEM_LIMIT)


def _dot(a, b):
    return jnp.dot(a.astype(BF16), b.astype(BF16), preferred_element_type=F32)


def _dot_nt(a, b):
    return lax.dot_general(a.astype(BF16), b.astype(BF16), (((1,), (1,)), ((), ())),
                           preferred_element_type=F32)


def _dot_tn(a, b):
    return lax.dot_general(a.astype(BF16), b.astype(BF16), (((0,), (0,)), ((), ())),
                           preferred_element_type=F32)


def _split2(x):
    hi = x.astype(BF16)
    lo = (x - hi.astype(F32)).astype(BF16)
    return hi, lo


def _split3(x):
    hi = x.astype(BF16)
    r1 = x - hi.astype(F32)
    mid = r1.astype(BF16)
    lo = (r1 - mid.astype(F32)).astype(BF16)
    return hi, mid, lo


def _dot_exact_rhs(x, m_bf16):
    hi, mid, lo = _split3(x)
    d = functools.partial(jnp.dot, preferred_element_type=F32)
    return d(hi, m_bf16) + d(mid, m_bf16) + d(lo, m_bf16)


def _dot_exact_lhs(m_bf16, x):
    hi, mid, lo = _split3(x)
    d = functools.partial(jnp.dot, preferred_element_type=F32)
    return d(m_bf16, hi) + d(m_bf16, mid) + d(m_bf16, lo)


def _rms(x, g):
    return x * lax.rsqrt(jnp.mean(x * x, axis=-1, keepdims=True) + RMS_EPS) * g


def _sigmoid(x):
    return 1.0 / (1.0 + jnp.exp(-x))


def _head_ones():
    i = lax.broadcasted_iota(jnp.int32, (C_A, C_A), 0) // HEAD
    j = lax.broadcasted_iota(jnp.int32, (C_A, C_A), 1) // HEAD
    return (i == j).astype(BF16)


def _pick_tile(n, mult, cap):
    best = mult
    t = mult
    while t <= min(n, cap):
        if n % t == 0:
            best = t
        t += mult
    assert n % best == 0, (n, mult)
    return best


def _in_proj_kernel(x_ref, g_ref, wa_ref, wb_ref, pa_ref, pb_ref):
    h = _rms(x_ref[...], g_ref[...]).astype(BF16)
    pa_ref[...] = jnp.dot(h, wa_ref[...], preferred_element_type=F32)
    pb_ref[...] = jnp.dot(h, wb_ref[...], preferred_element_type=F32)


def _in_proj(x, g, w_a, w_b):
    n = x.shape[0]
    tm = _pick_tile(n, 8, ROW_TILE)
    return pl.pallas_call(
        _in_proj_kernel,
        out_shape=(jax.ShapeDtypeStruct((n, SHIFT_COLS), F32), jax.ShapeDtypeStruct((n, 2 * C_B), F32)),
        grid=(n // tm,),
        in_specs=[pl.BlockSpec((tm, D_MODEL), lambda i: (i, 0)),
                  pl.BlockSpec((1, D_MODEL), lambda i: (0, 0)),
                  pl.BlockSpec((D_MODEL, SHIFT_COLS), lambda i: (0, 0)),
                  pl.BlockSpec((D_MODEL, 2 * C_B), lambda i: (0, 0))],
        out_specs=(pl.BlockSpec((tm, SHIFT_COLS), lambda i: (i, 0)),
                   pl.BlockSpec((tm, 2 * C_B), lambda i: (i, 0))),
        compiler_params=_cparams(("parallel",)),
        name="in_proj",
    )(x, g, w_a, w_b)


def _prep_math(p, pprev, mu, w0, a0, w_wa, w_g, k_k, k_a, ones_h, outs):
    r_ref, k_ref, v_ref, kk_ref, b_ref, lw_ref, g_ref = outs
    xs = p + (pprev - p) * mu
    r = xs[:, 0:C_A]
    k = xs[:, C_A:2 * C_A]
    v = xs[:, 2 * C_A:3 * C_A]
    lo = xs[:, 3 * C_A:3 * C_A + LORA_WA]
    g_lo = xs[:, 3 * C_A + LORA_WA:]
    lane = lax.broadcasted_iota(jnp.int32, lo.shape, 1)
    lo = jnp.where(lane < D_DECAY_LORA, jnp.tanh(lo), lo)
    wa = _dot(lo, w_wa)
    z = -(w0 + wa[:, :C_A])
    w = -(jnp.maximum(z, 0.0) + jnp.log(1.0 + jnp.exp(-jnp.abs(z)))) - 0.5
    lw_ref[...] = -jnp.exp(w)
    a = _sigmoid(a0 + wa[:, C_A:])
    g_ref[...] = _dot(_sigmoid(g_lo), w_g)
    kk = k * k_k
    hi, lo2 = _split2(kk * kk)
    ss = jnp.dot(hi, ones_h, preferred_element_type=F32) + jnp.dot(lo2, ones_h, preferred_element_type=F32)
    kk = kk / jnp.maximum(jnp.sqrt(ss), 1e-12)
    r_ref[...] = r
    v_ref[...] = v
    k_ref[...] = k * (1.0 + (a - 1.0) * k_a)
    kk_ref[...] = kk
    b_ref[...] = kk * a


def _prep_prompt_kernel(p_ref, halo_ref, mu_ref, w0_ref, a0_ref, wwa_ref, wg_ref, kk_ref_w, ka_ref, *outs):
    p = p_ref[...]
    prev_last = jnp.where(pl.program_id(1) == 0, 0.0, halo_ref[7:8, :])
    rolled = pltpu.roll(p, 1, 0)
    row = lax.broadcasted_iota(jnp.int32, p.shape, 0)
    pprev = jnp.where(row == 0, prev_last, rolled)
    _prep_math(p, pprev, mu_ref[...], w0_ref[...], a0_ref[...], wwa_ref[...], wg_ref[...],
               kk_ref_w[...], ka_ref[...], _head_ones(), outs)


def _prep_sample_kernel(p_ref, shift_ref, mu_ref, w0_ref, a0_ref, wwa_ref, wg_ref, kk_ref_w, ka_ref,
                        *outs_and_scratch):
    outs = outs_and_scratch
    p = p_ref[...]
    sb = shift_ref.shape[0]
    t_len = p.shape[0] // sb
    first = jnp.broadcast_to(shift_ref[...][:, None, :], (sb, t_len, SHIFT_COLS)).reshape(p.shape)
    row = lax.broadcasted_iota(jnp.int32, p.shape, 0)
    pprev = jnp.where(row % t_len == 0, first, pltpu.roll(p, 1, 0))
    _prep_math(p, pprev, mu_ref[...], w0_ref[...], a0_ref[...], wwa_ref[...], wg_ref[...],
               kk_ref_w[...], ka_ref[...], _head_ones(), outs)


def _prep_weight_specs(nd):
    z = (lambda *_: (0, 0))
    del nd
    return [pl.BlockSpec((1, SHIFT_COLS), z), pl.BlockSpec((1, C_A), z), pl.BlockSpec((1, C_A), z),
            pl.BlockSpec((LORA_WA, 2 * C_A), z), pl.BlockSpec((D_GATE_LORA, C_A), z),
            pl.BlockSpec((1, C_A), z), pl.BlockSpec((1, C_A), z)]


def _rwkv_prep_prompt(p_a, row0, n_seq, t_len, wts):
    tm = _pick_tile(t_len, 8, 704)
    nt = t_len // tm
    assert row0 % tm == 0
    base = row0 // tm
    out = jax.ShapeDtypeStruct((n_seq * t_len, C_A), F32)
    ospec = pl.BlockSpec((tm, C_A), lambda b, c: (b * nt + c, 0))
    return pl.pallas_call(
        _prep_prompt_kernel,
        out_shape=(out,) * 7,
        grid=(n_seq, nt),
        in_specs=[pl.BlockSpec((tm, SHIFT_COLS), lambda b, c: (base + b * nt + c, 0)),
                  pl.BlockSpec((8, SHIFT_COLS),
                               lambda b, c: (jnp.maximum((base + b * nt + c) * (tm // 8) - 1, 0), 0)),
                  ] + _prep_weight_specs(2),
        out_specs=(ospec,) * 7,
        compiler_params=_cparams(("parallel", "parallel")),
        name="rwkv_prep_prompt",
    )(p_a, p_a, *wts)


def _rwkv_prep_sample(p_a, row0, n_seq, t_len, shift, wts):
    sb = _pick_tile(n_seq, 8, 64)
    tm = sb * t_len
    assert row0 % tm == 0 and t_len % 8 == 0
    base = row0 // tm
    out = jax.ShapeDtypeStruct((n_seq * t_len, C_A), F32)
    ospec = pl.BlockSpec((tm, C_A), lambda i: (i, 0))
    return pl.pallas_call(
        _prep_sample_kernel,
        out_shape=(out,) * 7,
        grid=(n_seq // sb,),
        in_specs=[pl.BlockSpec((tm, SHIFT_COLS), lambda i: (base + i, 0)),
                  pl.BlockSpec((sb, SHIFT_COLS), lambda i: (i, 0)),
                  ] + _prep_weight_specs(1),
        out_specs=(ospec,) * 7,
        compiler_params=_cparams(("parallel",)),
        name="rwkv_prep_sample",
    )(p_a, shift, *wts)


def _scan_kernel(has_s0, unroll, *refs):
    if has_s0:
        (r_ref, k_ref, v_ref, kk_ref, b_ref, lw_ref, g_ref, s0_ref,
         lnw_ref, lnb_ref, rk_ref, y_ref, s_ref) = refs
    else:
        (r_ref, k_ref, v_ref, kk_ref, b_ref, lw_ref, g_ref,
         lnw_ref, lnb_ref, rk_ref, y_ref, s_ref) = refs
        s0_ref = None
    n_seq, c_len, _ = r_ref.shape

    @pl.when(pl.program_id(1) == 0)
    def _():
        if has_s0:
            s_ref[...] = s0_ref[...]
        else:
            s_ref[...] = jnp.zeros_like(s_ref)

    ti = lax.broadcasted_iota(jnp.int32, (c_len, c_len), 0)
    si = lax.broadcasted_iota(jnp.int32, (c_len, c_len), 1)
    tri_incl = (si <= ti).astype(BF16)
    strict = (si < ti).astype(F32)
    incl = (si <= ti).astype(F32)
    incl_signed = jnp.concatenate([incl, -incl], axis=1)
    ones_h = _head_ones()
    lnw, lnb, rk = lnw_ref[...], lnb_ref[...], rk_ref[...]
    heads = range(N_HEADS)
    hsl = [slice(h * HEAD, (h + 1) * HEAD) for h in heads]

    def gsum(x):
        hi, lo = _split2(x)
        return jnp.dot(hi, ones_h, preferred_element_type=F32) + jnp.dot(lo, ones_h, preferred_element_type=F32)

    def load(s):
        return ([ref[s] for ref in (r_ref, k_ref, v_ref, kk_ref, b_ref, lw_ref, g_ref)],
                [s_ref[s, h] for h in heads])

    def compute(streams, states):
        r, k, v, kk, b, lw, g = streams
        cum = _dot_exact_lhs(tri_incl, lw)
        tot = cum[c_len - 1:c_len, :]
        g_inv = jnp.exp(-cum)
        g_end = jnp.exp(tot - cum)
        l_mat = jnp.concatenate([kk * jnp.exp(cum - lw), r * jnp.exp(cum)], axis=0).astype(BF16)
        kbh = jnp.concatenate([k * g_inv, b * g_inv], axis=0).astype(BF16)
        e_mat = jnp.concatenate([k * g_end, -(b * g_end)], axis=0).astype(BF16)
        g_tot = jnp.exp(tot)
        l_h = [l_mat[:, hs] for hs in hsl]
        v_h = [v[:, hs] for hs in hsl]
        n_ub = [_dot_nt(l_h[h][:c_len], kbh[c_len:, hsl[h]]) * strict for h in heads]
        a_vk = [_dot_nt(l_h[h][:c_len], kbh[:c_len, hsl[h]]) * strict for h in heads]
        pm = [_dot_nt(l_h[h], states[h]) for h in heads]
        x = [pm[h][:c_len] + _dot(a_vk[h], v_h[h]) for h in heads]
        x = [x[h] - _dot(n_ub[h], x[h]) for h in heads]
        pw = n_ub
        m = 2
        while m < c_len:
            pw = [_dot(p, p) for p in pw]
            x = [x[h] + _dot(pw[h], x[h]) for h in heads]
            m *= 2
        b_m = [_dot_nt(l_h[h][c_len:], kbh[:, hsl[h]]) * incl_signed for h in heads]
        vu = [jnp.concatenate([v_h[h], x[h]], axis=0) for h in heads]
        y = jnp.concatenate([pm[h][c_len:] + _dot(b_m[h], vu[h]) for h in heads], axis=1)
        new_states = [states[h] * g_tot[:, hsl[h]] + _dot_tn(vu[h], e_mat[:, hsl[h]]) for h in heads]
        mu = gsum(y) * (1.0 / HEAD)
        d = y - mu
        var = gsum(d * d) * (1.0 / HEAD)
        yn = d * lax.rsqrt(var + GN_EPS) * lnw + lnb
        bonus = gsum(r * k * rk) * v
        return (yn + bonus) * g, new_states

    def group(base):
        idx = [base + i for i in range(unroll)]
        loaded = [load(s) for s in idx]
        results = [compute(*d) for d in loaded]
        for s, (y, new_states) in zip(idx, results):
            y_ref[s] = y
            for h in heads:
                s_ref[s, h] = new_states[h]

    if n_seq == unroll:
        group(0)
    else:
        def body(i, carry):
            group(i * unroll)
            return carry
        lax.fori_loop(0, n_seq // unroll, body, 0)


def _wkv_scan(streams, s0, n_seq, t_len, c_len, sb, unroll, lnw, lnb, rk):
    nc = t_len // c_len
    streams = [a.reshape(n_seq, t_len, C_A) for a in streams]
    dspec = pl.BlockSpec((sb, c_len, C_A), lambda i, c: (i, c, 0))
    sspec = pl.BlockSpec((sb, N_HEADS, HEAD, HEAD), lambda i, c: (i, 0, 0, 0))
    wspec = pl.BlockSpec((1, C_A), lambda i, c: (0, 0))
    ins = list(streams) + ([s0] if s0 is not None else []) + [lnw, lnb, rk]
    in_specs = [dspec] * 7 + ([sspec] if s0 is not None else []) + [wspec] * 3
    y, s_new = pl.pallas_call(
        functools.partial(_scan_kernel, s0 is not None, unroll),
        out_shape=(jax.ShapeDtypeStruct((n_seq, t_len, C_A), F32),
                   jax.ShapeDtypeStruct((n_seq, N_HEADS, HEAD, HEAD), F32)),
        grid=(n_seq // sb, nc),
        in_specs=in_specs,
        out_specs=(dspec, sspec),
        compiler_params=_cparams(("parallel", "arbitrary")),
        name="wkv_scan_c%d" % c_len,
    )(*ins)
    return y.reshape(n_seq * t_len, C_A), s_new


def _conv_tail(z, dwb, lnw, lnb):
    z = z + dwb
    mu = jnp.mean(z, axis=-1, keepdims=True)
    d = z - mu
    var = jnp.mean(d * d, axis=-1, keepdims=True)
    z = d * lax.rsqrt(var + LN_EPS) * lnw + lnb
    return z * _sigmoid(z)


def _glu(pb, glu_b):
    u = pb + glu_b
    return u[:, :C_B] * _sigmoid(u[:, C_B:])


def _conv_prompt_kernel(n_pad, pb_ref, halo_ref, glub_ref, dw_ref, dwb_ref, lnw_ref, lnb_ref,
                        y_ref, st_ref, ext_ref):
    c = pl.program_id(1)
    tm = pb_ref.shape[0]
    glub = glub_ref[...]
    halo = jnp.where(c == 0, 0.0, _glu(halo_ref[...], glub))
    u = _glu(pb_ref[...], glub)
    t_glob = c * tm + lax.broadcasted_iota(jnp.int32, u.shape, 0)
    u = jnp.where(t_glob < n_pad, 0.0, u)
    ext_ref[0:CONV_HALO, :] = halo
    ext_ref[CONV_HALO:, :] = u
    off = CONV_HALO - (CONV_W - 1)
    rb = 32

    def blk(i, carry):
        base = pl.multiple_of(i * rb, rb)
        win = ext_ref[pl.ds(base, rb + CONV_HALO), :]
        acc = jnp.zeros((rb, C_B), F32)
        for s in range(8):
            taps = [j for j in range(CONV_W) if (off + j) % 8 == s]
            shifted = win if s == 0 else pltpu.roll(win, win.shape[0] - s, 0)
            for j in taps:
                a = off + j - s
                acc = acc + shifted[a:a + rb] * dw_ref[j:j + 1, :]
        y_ref[pl.ds(base, rb), :] = _conv_tail(acc, dwb_ref[...], lnw_ref[...], lnb_ref[...])
        return carry

    lax.fori_loop(0, tm // rb, blk, 0)

    @pl.when(c == pl.num_programs(1) - 1)
    def _():
        st_ref[0] = ext_ref[CONV_HALO + tm - (CONV_W - 1):CONV_HALO + tm, :]


def _conv_prompt(p_b, row0, n_seq, t_len, n_pad, wts):
    tm = _pick_tile(t_len, CONV_HALO, 704)
    nt = t_len // tm
    assert row0 % tm == 0
    base = row0 // tm
    z2 = lambda b, c: (0, 0)
    return pl.pallas_call(
        functools.partial(_conv_prompt_kernel, n_pad),
        out_shape=(jax.ShapeDtypeStruct((n_seq * t_len, C_B), F32),
                   jax.ShapeDtypeStruct((n_seq, CONV_W - 1, C_B), F32)),
        grid=(n_seq, nt),
        in_specs=[pl.BlockSpec((tm, 2 * C_B), lambda b, c: (base + b * nt + c, 0)),
                  pl.BlockSpec((CONV_HALO, 2 * C_B),
                               lambda b, c: (jnp.maximum((base + b * nt + c) * (tm // CONV_HALO) - 1, 0), 0)),
                  pl.BlockSpec((1, 2 * C_B), z2), pl.BlockSpec((CONV_W, C_B), z2),
                  pl.BlockSpec((1, C_B), z2), pl.BlockSpec((1, C_B), z2), pl.BlockSpec((1, C_B), z2)],
        out_specs=(pl.BlockSpec((tm, C_B), lambda b, c: (b * nt + c, 0)),
                   pl.BlockSpec((1, CONV_W - 1, C_B), lambda b, c: (b, 0, 0))),
        scratch_shapes=[pltpu.VMEM((CONV_HALO + tm, C_B), F32)],
        compiler_params=_cparams(("parallel", "arbitrary")),
        name="conv_prompt",
    )(p_b, p_b, *wts)


def _conv_sample_kernel(pb_ref, buf_ref, glub_ref, dw_ref, dwb_ref, lnw_ref, lnb_ref,
                        y_ref, st_ref, ext_ref):
    sb, t_len = buf_ref.shape[0], pb_ref.shape[0] // buf_ref.shape[0]
    hist = CONV_W - 1
    u = _glu(pb_ref[...], glub_ref[...])
    lead = ext_ref.shape[1] - hist - t_len
    ext_ref[:, lead:lead + hist, :] = buf_ref[...]
    ext_ref[:, lead + hist:, :] = u.reshape(sb, t_len, C_B)
    acc = jnp.zeros((sb, t_len, C_B), F32)
    for j in range(CONV_W):
        acc = acc + ext_ref[:, lead + j:lead + j + t_len, :] * dw_ref[j:j + 1, :]
    z = _conv_tail(acc.reshape(sb * t_len, C_B), dwb_ref[...], lnw_ref[...], lnb_ref[...])
    y_ref[...] = z
    st_ref[...] = ext_ref[:, lead + t_len:, :]


def _conv_sample(p_b, row0, n_seq, t_len, buf, wts):
    sb = _pick_tile(n_seq, 8, 32)
    tm = sb * t_len
    assert row0 % tm == 0 and t_len % 8 == 0
    base = row0 // tm
    hist = CONV_W - 1
    ext_rows = -(-(hist + t_len) // 8) * 8
    z1 = lambda i: (0, 0)
    return pl.pallas_call(
        _conv_sample_kernel,
        out_shape=(jax.ShapeDtypeStruct((n_seq * t_len, C_B), F32),
                   jax.ShapeDtypeStruct((n_seq, hist, C_B), F32)),
        grid=(n_seq // sb,),
        in_specs=[pl.BlockSpec((tm, 2 * C_B), lambda i: (base + i, 0)),
                  pl.BlockSpec((sb, hist, C_B), lambda i: (i, 0, 0)),
                  pl.BlockSpec((1, 2 * C_B), z1), pl.BlockSpec((CONV_W, C_B), z1),
                  pl.BlockSpec((1, C_B), z1), pl.BlockSpec((1, C_B), z1), pl.BlockSpec((1, C_B), z1)],
        out_specs=(pl.BlockSpec((tm, C_B), lambda i: (i, 0)),
                   pl.BlockSpec((sb, hist, C_B), lambda i: (i, 0, 0))),
        scratch_shapes=[pltpu.VMEM((sb, ext_rows, C_B), F32)],
        compiler_params=_cparams(("parallel",)),
        name="conv_sample",
    )(p_b, buf, *wts)


TILE_SUB = D_MODEL // LANES


def _store_token_tiles(ref, x, lead=()):
    rows = x.shape[0]
    for c in range(TILE_SUB):
        ref[lead + (pl.ds(c, rows, stride=TILE_SUB), slice(None))] = x[:, c * LANES:(c + 1) * LANES]


def _load_token_tiles(ref, rows, lead=()):
    return jnp.concatenate([ref[lead + (pl.ds(c, rows, stride=TILE_SUB), slice(None))]
                            for c in range(TILE_SUB)], axis=1)


def _out_route_kernel(n_first, yaf_ref, yas_ref, ybf_ref, ybs_ref, x_ref, woa_ref, wob_ref, g_ref,
                      wrh_ref, wrl_ref, br_ref, x1_ref, h2_ref, idx_ref, gate_ref, rank_ref, cnt_ref, run_ref):
    i = pl.program_id(0)
    first = i < n_first
    ya = jnp.where(first, yaf_ref[...], yas_ref[...])
    yb = jnp.where(first, ybf_ref[...], ybs_ref[...])
    x1 = x_ref[...] + _dot(ya, woa_ref[...]) + _dot(yb, wob_ref[...])
    x1_ref[...] = x1
    h2 = _rms(x1, g_ref[...])
    _store_token_tiles(h2_ref, h2)
    hi, lo = _split2(h2)
    d = functools.partial(jnp.dot, preferred_element_type=F32)
    logits = d(hi, wrh_ref[...]) + d(hi, wrl_ref[...]) + d(lo, wrh_ref[...]) + br_ref[...]
    lane = lax.broadcasted_iota(jnp.int32, logits.shape, 1)
    logits = jnp.where(lane < N_EXPERTS, logits, -jnp.inf)
    idx_out = jnp.zeros(logits.shape, jnp.int32)
    val_out = jnp.full(logits.shape, -jnp.inf, F32)
    onehots = []
    for kk in range(TOP_K):
        m = jnp.max(logits, axis=-1, keepdims=True)
        sel = jnp.min(jnp.where(logits == m, lane, LANES), axis=-1, keepdims=True)
        idx_out = jnp.where(lane == kk, sel, idx_out)
        val_out = jnp.where(lane == kk, m, val_out)
        hit = lane == sel
        onehots.append(hit.astype(F32))
        logits = jnp.where(hit, -jnp.inf, logits)
    e = jnp.exp(val_out - jnp.max(val_out, axis=-1, keepdims=True))
    idx_ref[...] = idx_out
    gate_ref[...] = e / jnp.sum(e, axis=-1, keepdims=True)

    @pl.when(i == 0)
    def _():
        run_ref[...] = jnp.zeros_like(run_ref)

    tm = logits.shape[0]
    total = onehots[0] + onehots[1] + onehots[2] + onehots[3]
    ti = lax.broadcasted_iota(jnp.int32, (tm, tm), 0)
    si = lax.broadcasted_iota(jnp.int32, (tm, tm), 1)
    before = jnp.dot((si < ti).astype(BF16), total.astype(BF16), preferred_element_type=F32) + run_ref[...]
    rank = jnp.zeros(logits.shape, F32)
    for kk in range(TOP_K):
        rank = jnp.where(lane == kk, jnp.sum(onehots[kk] * before, axis=-1, keepdims=True), rank)
    rank_ref[...] = (rank.astype(jnp.int32) << EXPERT_BITS) | idx_out
    run_ref[...] += jnp.sum(total, axis=0, keepdims=True)
    cnt_ref[...] = run_ref[...]


def _out_route(ya_f, ya_s, yb_f, yb_s, x, wo_a, wo_b, g, wr_hi, wr_lo, b_r):
    n = x.shape[0]
    n_f = ya_f.shape[0]
    tm = ROW_TILE
    while n_f % tm or (n - n_f) % tm:
        tm //= 2
    assert tm % 8 == 0
    nf_t, ns_t = n_f // tm, (n - n_f) // tm
    z = lambda i: (0, 0)
    row = lambda w: pl.BlockSpec((tm, w), lambda i: (i, 0))
    fst = lambda w: pl.BlockSpec((tm, w), lambda i: (jnp.minimum(i, nf_t - 1), 0))
    snd = lambda w: pl.BlockSpec((tm, w), lambda i: (jnp.maximum(i - nf_t, 0), 0))
    return pl.pallas_call(
        functools.partial(_out_route_kernel, nf_t),
        out_shape=(jax.ShapeDtypeStruct((n, D_MODEL), F32), jax.ShapeDtypeStruct((n * TILE_SUB, LANES), F32),
                   jax.ShapeDtypeStruct((n, LANES), jnp.int32), jax.ShapeDtypeStruct((n, LANES), F32),
                   jax.ShapeDtypeStruct((n, LANES), jnp.int32), jax.ShapeDtypeStruct((1, LANES), F32)),
        grid=(nf_t + ns_t,),
        in_specs=[fst(C_A), snd(C_A), fst(C_B), snd(C_B), row(D_MODEL),
                  pl.BlockSpec((C_A, D_MODEL), z), pl.BlockSpec((C_B, D_MODEL), z),
                  pl.BlockSpec((1, D_MODEL), z),
                  pl.BlockSpec((D_MODEL, LANES), z), pl.BlockSpec((D_MODEL, LANES), z),
                  pl.BlockSpec((1, LANES), z)],
        out_specs=(row(D_MODEL), pl.BlockSpec((tm * TILE_SUB, LANES), lambda i: (i, 0)), row(LANES), row(LANES),
                   row(LANES), pl.BlockSpec((1, LANES), z)),
        scratch_shapes=[pltpu.VMEM((1, LANES), F32)],
        compiler_params=_cparams(("arbitrary",)),
        name="out_route",
    )(ya_f, ya_s, yb_f, yb_s, x, wo_a, wo_b, g, wr_hi, wr_lo, b_r)


FFN_BUFS = 3


def _ffn_kernel(n_tok, asg_ref, bexp_ref, nblk_ref, h2_hbm, wg_ref, bg_ref, wu_ref, bu_ref, wd_ref, bd_ref,
                out_hbm, *scratch):
    xbufs, ybufs = scratch[:FFN_BUFS], scratch[FFN_BUFS:2 * FFN_BUFS]
    wg_bf, wu_bf, wd_bf, gsem, ssem = scratch[2 * FFN_BUFS:]
    j = pl.program_id(0)
    last = pl.num_programs(0) - 1
    n_used = nblk_ref[0]
    rows = MOE_ROWS

    def gather(i, tab_row, s):
        tok = jnp.minimum(lax.shift_right_logical(asg_ref[tab_row, i], jnp.int32(2)), n_tok - 1)
        return pltpu.make_async_copy(h2_hbm.at[pl.ds(pl.multiple_of(tok * TILE_SUB, TILE_SUB), TILE_SUB), :],
                                     xbufs[s].at[pl.ds(i * TILE_SUB, TILE_SUB), :], gsem.at[s])

    def scatter(i, tab_row, s):
        dst = asg_ref[tab_row, i]
        return pltpu.make_async_copy(ybufs[s].at[pl.ds(i * TILE_SUB, TILE_SUB), :],
                                     out_hbm.at[pl.ds(pl.multiple_of(dst * TILE_SUB, TILE_SUB), TILE_SUB), :],
                                     ssem.at[s])

    def start_all(fn, unrolled):
        if unrolled:
            for i in range(rows):
                fn(i).start(priority=i % 2)
        else:
            def body(i, carry):
                fn(i).start()
                return carry
            lax.fori_loop(0, rows, body, 0)

    def wait_gather(s):
        pltpu.make_async_copy(h2_hbm.at[pl.ds(0, rows * TILE_SUB), :], xbufs[s], gsem.at[s]).wait()

    def wait_scatter(s):
        pltpu.make_async_copy(ybufs[s], out_hbm.at[pl.ds(0, rows * TILE_SUB), :], ssem.at[s]).wait()

    @pl.when(j == 0)
    def _():
        for yb in ybufs:
            yb[...] = jnp.zeros_like(yb)
        start_all(lambda i: gather(i, 1, 0), False)
        start_all(lambda i: gather(i, 2, 1), False)

    def step(slot):
        prev = (slot + FFN_BUFS - 1) % FFN_BUFS
        nxt = (slot + 1) % FFN_BUFS

        def start_next(unrolled):
            start_all(lambda i: gather(i, j + 3, prev), unrolled)
            start_all(lambda i: scatter(i, j, prev), unrolled)

        @pl.when(j >= 2)
        def _():
            wait_scatter(slot)

        wait_gather(slot)

        @pl.when(j < n_used)
        def _():
            new_expert = jnp.logical_or(j == 0, bexp_ref[j] != bexp_ref[jnp.maximum(j - 1, 0)])

            @pl.when(new_expert)
            def _():
                wg_bf[...] = wg_ref[0].astype(BF16)
                wu_bf[...] = wu_ref[0].astype(BF16)
                wd_bf[...] = wd_ref[0].astype(BF16)

            x = _load_token_tiles(xbufs[slot], rows).astype(BF16)
            start_next(True)
            gt = jnp.minimum(jnp.dot(x, wg_bf[...], preferred_element_type=F32) + bg_ref[0], SWIGLU_LIMIT)
            up = jnp.clip(jnp.dot(x, wu_bf[...], preferred_element_type=F32) + bu_ref[0],
                          -SWIGLU_LIMIT, SWIGLU_LIMIT)
            act = (up + 1.0) * gt * _sigmoid(SWIGLU_ALPHA * gt)
            y = jnp.dot(act.astype(BF16), wd_bf[...], preferred_element_type=F32) + bd_ref[0]
            _store_token_tiles(ybufs[slot], y)

        @pl.when(j >= n_used)
        def _():
            start_next(False)

        @pl.when(j == last)
        def _():
            for s in (nxt, prev):
                wait_gather(s)
                wait_scatter(s)

    for s in range(FFN_BUFS):
        pl.when(lax.rem(j, FFN_BUFS) == s)(functools.partial(step, s))


def _moe_ffn(asg, block_expert, n_used, h2t, w_gate, b_gate, w_up, b_up, w_down, b_down):
    n_steps = block_expert.shape[0]
    n_tok = h2t.shape[0] // TILE_SUB
    d_ff = w_gate.shape[2]
    out_rows = n_steps * MOE_ROWS
    wspec = lambda shape: pl.BlockSpec((1,) + shape, lambda j, a, be, nb: (be[j], 0, 0))
    tiles = (MOE_ROWS * TILE_SUB, LANES)
    return pl.pallas_call(
        functools.partial(_ffn_kernel, n_tok),
        out_shape=jax.ShapeDtypeStruct((out_rows * TILE_SUB, LANES), F32),
        grid_spec=pltpu.PrefetchScalarGridSpec(
            num_scalar_prefetch=3,
            grid=(n_steps,),
            in_specs=[pl.BlockSpec(memory_space=pl.ANY),
                      wspec((D_MODEL, d_ff)), wspec((1, d_ff)),
                      wspec((D_MODEL, d_ff)), wspec((1, d_ff)),
                      wspec((d_ff, D_MODEL)), wspec((1, D_MODEL))],
            out_specs=pl.BlockSpec(memory_space=pl.ANY),
            scratch_shapes=[pltpu.VMEM(tiles, F32)] * (2 * FFN_BUFS) + [
                pltpu.VMEM((D_MODEL, d_ff), BF16), pltpu.VMEM((D_MODEL, d_ff), BF16),
                pltpu.VMEM((d_ff, D_MODEL), BF16),
                pltpu.SemaphoreType.DMA((FFN_BUFS,)), pltpu.SemaphoreType.DMA((FFN_BUFS,))]),
        compiler_params=_cparams(("arbitrary",)),
        name="moe_ffn",
    )(asg, block_expert, n_used, h2t, w_gate, b_gate.reshape(N_EXPERTS, 1, d_ff),
      w_up, b_up.reshape(N_EXPERTS, 1, d_ff), w_down, b_down.reshape(N_EXPERTS, 1, D_MODEL))


def _combine_kernel(e_ref, x1_ref, gate_ref, g_ref, y_ref):
    tm = x1_ref.shape[0]
    gates = gate_ref[...]
    acc = x1_ref[...]
    for kk in range(TOP_K):
        yk = jnp.concatenate([e_ref[pl.ds(kk * TILE_SUB + c, tm, stride=TOP_K * TILE_SUB), :]
                              for c in range(TILE_SUB)], axis=1)
        acc = acc + yk * gates[:, kk:kk + 1]
    y_ref[...] = _rms(acc, g_ref[...])


def _combine(yt, x1, gates, g_final):
    n = x1.shape[0]
    tm = _pick_tile(n, 8, 256)
    return pl.pallas_call(
        _combine_kernel,
        out_shape=jax.ShapeDtypeStruct((n, D_MODEL), F32),
        grid=(n // tm,),
        in_specs=[pl.BlockSpec((tm * TOP_K * TILE_SUB, LANES), lambda i: (i, 0)),
                  pl.BlockSpec((tm, D_MODEL), lambda i: (i, 0)),
                  pl.BlockSpec((tm, LANES), lambda i: (i, 0)),
                  pl.BlockSpec((1, D_MODEL), lambda i: (0, 0))],
        out_specs=pl.BlockSpec((tm, D_MODEL), lambda i: (i, 0)),
        compiler_params=_cparams(("parallel",)),
        name="moe_combine",
    )(yt, x1, gates, g_final)


def _routing(idx, rank, counts):
    n = idx.shape[0]
    n_assign = n * TOP_K
    del idx
    packed = rank[:, :TOP_K].reshape(-1)
    flat_e = packed & ((1 << EXPERT_BITS) - 1)
    rank = packed >> EXPERT_BITS
    counts = counts[0, :N_EXPERTS].astype(jnp.int32)
    padded = (counts + MOE_ROWS - 1) // MOE_ROWS * MOE_ROWS
    pad_end = jnp.cumsum(padded)
    pad_start = pad_end - padded
    n_blocks = (n_assign + N_EXPERTS * (MOE_ROWS - 1) + MOE_ROWS - 1) // MOE_ROWS
    dest = (pad_start[flat_e] + rank).astype(jnp.int32) + MOE_ROWS
    tab_rows = n_blocks + 1 + FFN_BUFS
    q = jnp.arange(tab_rows * MOE_ROWS, dtype=jnp.int32) - MOE_ROWS
    e_blk = jnp.sum((q[::MOE_ROWS, None] >= pad_end[None, :]).astype(jnp.int32), axis=1)
    e_of = jnp.broadcast_to(e_blk[:, None], (tab_rows, MOE_ROWS)).reshape(-1)
    waste = padded - counts
    waste_before = jnp.concatenate([jnp.zeros((1,), waste.dtype), jnp.cumsum(waste)])
    e_c = jnp.minimum(e_of, N_EXPERTS - 1)
    in_expert = MOE_ROWS + waste_before[e_c] + (q - pad_start[e_c] - counts[e_c])
    after = MOE_ROWS + waste_before[N_EXPERTS] + (q - pad_end[N_EXPERTS - 1])
    dump = jnp.where(q < 0, q + MOE_ROWS, jnp.where(e_of < N_EXPERTS, in_expert, after))
    asg = (n_assign + dump).astype(jnp.int32).at[dest].set(
        jnp.arange(n_assign, dtype=jnp.int32), unique_indices=True, mode='promise_in_bounds')
    n_used = (pad_end[-1] // MOE_ROWS).astype(jnp.int32)
    blk_start = jnp.arange(n_blocks + 1, dtype=jnp.int32) * MOE_ROWS
    block_expert = jnp.minimum(jnp.sum((blk_start[:, None] >= pad_end[None, :]).astype(jnp.int32), axis=1),
                               N_EXPERTS - 1)
    last_used = block_expert[jnp.maximum(n_used - 1, 0)]
    block_expert = jnp.where(jnp.arange(n_blocks + 1) < n_used, block_expert, last_used).astype(jnp.int32)
    return asg.reshape(tab_rows, MOE_ROWS), block_expert, n_used.reshape(1)


def _forward(x_prompt, x_sample, state_wkv, state_shift, state_conv, meta_tokens, norm_mix, w_in,
             tshift_mu, decay_w0, decay_w2, iclr_a0, iclr_a2, gate_g2, k_k, k_a, r_k, lnx_w, lnx_b,
             glu_b, dw_weight, dw_bias, conv_ln_w, conv_ln_b, w_out, norm_ffn, w_router, b_router,
             w_gate, b_gate, w_up, b_up, w_down, b_down, norm_final):
    n_p, seq, _ = x_prompt.shape
    n_s, t_s, _ = x_sample.shape
    t_real = N_META + seq
    n_pad = (-t_real) % CHUNK
    t_p = t_real + n_pad
    rows_p = n_p * t_p
    rows_s = n_s * t_s
    lyr = 0

    lead = jnp.concatenate([jnp.zeros((n_pad, D_MODEL), F32), meta_tokens.astype(F32)], axis=0)
    pieces = []
    for b in range(n_p):
        pieces += [lead, x_prompt[b]]
    x_all = jnp.concatenate(pieces + [x_sample.reshape(rows_s, D_MODEL)], axis=0)

    row = lambda a: a[lyr].reshape(1, -1).astype(F32)
    w_in_bf = w_in[lyr].astype(BF16)
    p_a, p_b = _in_proj(x_all, row(norm_mix), w_in_bf[:, :SHIFT_COLS], w_in_bf[:, SHIFT_COLS:])

    w_wa = jnp.zeros((LORA_WA, 2 * C_A), F32)
    w_wa = w_wa.at[:D_DECAY_LORA, :C_A].set(decay_w2[lyr]).at[D_DECAY_LORA:, C_A:].set(iclr_a2[lyr])
    prep_w = (row(tshift_mu), row(decay_w0), row(iclr_a0), w_wa.astype(BF16), gate_g2[lyr].astype(BF16),
              row(k_k), row(k_a))
    st_p = _rwkv_prep_prompt(p_a, 0, n_p, t_p, prep_w)
    st_s = _rwkv_prep_sample(p_a, rows_p, n_s, t_s, state_shift[lyr], prep_w)

    scan_w = (row(lnx_w), row(lnx_b), row(r_k))
    sb_p = _pick_tile(n_p, 1, SCAN_UNROLL)
    ya_p, wkv_p = _wkv_scan(st_p, None, n_p, t_p, CHUNK, sb_p, sb_p, *scan_w)
    sb_s = _pick_tile(n_s, 1, 16)
    ya_s, wkv_s = _wkv_scan(st_s, state_wkv[lyr], n_s, t_s, t_s, sb_s, _pick_tile(sb_s, 1, SCAN_UNROLL), *scan_w)

    conv_w = (row(glu_b), dw_weight[lyr], row(dw_bias), row(conv_ln_w), row(conv_ln_b))
    yb_p, conv_p = _conv_prompt(p_b, 0, n_p, t_p, n_pad, conv_w)
    yb_s, conv_s = _conv_sample(p_b, rows_p, n_s, t_s, state_conv[lyr], conv_w)

    w_out_bf = w_out[lyr].astype(BF16)
    wr = jnp.zeros((D_MODEL, LANES), F32).at[:, :N_EXPERTS].set(w_router[lyr])
    wr_hi = wr.astype(BF16)
    wr_lo = (wr - wr_hi.astype(F32)).astype(BF16)
    b_r = jnp.zeros((1, LANES), F32).at[0, :N_EXPERTS].set(b_router[lyr])
    x1, h2, idx, gates, rank, counts = _out_route(ya_p, ya_s, yb_p, yb_s, x_all, w_out_bf[:C_A], w_out_bf[C_A:],
                                                  row(norm_ffn), wr_hi, wr_lo, b_r)

    asg, block_expert, n_used = _routing(idx, rank, counts)
    yt = _moe_ffn(asg, block_expert, n_used, h2, w_gate[lyr], b_gate[lyr], w_up[lyr], b_up[lyr],
                  w_down[lyr], b_down[lyr])
    y = _combine(yt, x1, gates, norm_final.reshape(1, -1))

    y_prompt = y[:rows_p].reshape(n_p, t_p, D_MODEL)[:, n_pad + N_META:]
    y_sample = y[rows_p:].reshape(n_s, t_s, D_MODEL)
    shift_p = p_a[t_p - 1:rows_p:t_p]
    shift_s = p_a[rows_p + t_s - 1::t_s]
    return (y_prompt.astype(x_prompt.dtype), y_sample.astype(x_sample.dtype),
            wkv_p[None], shift_p[None], conv_p[None], wkv_s[None], shift_s[None], conv_s[None])


def kernel(x_prompt, x_sample, state_wkv, state_shift, state_conv, meta_tokens, norm_mix, w_in, tshift_mu, decay_w0, decay_w2, iclr_a0, iclr_a2, gate_g2, k_k, k_a, r_k, lnx_w, lnx_b, glu_b, dw_weight, dw_bias, conv_ln_w, conv_ln_b, w_out, norm_ffn, w_router, b_router, w_gate, b_gate, w_up, b_up, w_down, b_down, norm_final):
    assert w_in.shape[0] == 1, "single trunk layer"
    return _forward(x_prompt, x_sample, state_wkv, state_shift, state_conv, meta_tokens, norm_mix, w_in,
                    tshift_mu, decay_w0, decay_w2, iclr_a0, iclr_a2, gate_g2, k_k, k_a, r_k, lnx_w, lnx_b,
                    glu_b, dw_weight, dw_bias, conv_ln_w, conv_ln_b, w_out, norm_ffn, w_router, b_router,
                    w_gate, b_gate, w_up, b_up, w_down, b_down, norm_final)
```

```python
import functools

import jax
import jax.numpy as jnp
from jax import lax
from jax.experimental import pallas as pl
from jax.experimental.pallas import tpu as pltpu

F32 = jnp.float32
BF16 = jnp.bfloat16

D_MODEL = 1024
N_META = 16
C_A = 512
HEAD = 64
N_HEADS = C_A // HEAD
C_B = 512
CONV_W = 31
D_DECAY_LORA = 64
D_AAA_LORA = 64
D_GATE_LORA = 128
LORA_WA = D_DECAY_LORA + D_AAA_LORA
SHIFT_COLS = 3 * C_A + LORA_WA + D_GATE_LORA
N_EXPERTS = 32
TOP_K = 4
SWIGLU_LIMIT = 7.0
SWIGLU_ALPHA = 1.702
RMS_EPS = 1e-5
LN_EPS = 1e-5
GN_EPS = 64e-5

LANES = 128
CHUNK = 64
SCAN_UNROLL = 2
CONV_HALO = 32
ROW_TILE = 512
MOE_ROWS = 256
VMEM_LIMIT = 56 * 1024 * 1024


def _cparams(sem):
    return pltpu.CompilerParams(dimension_semantics=sem, vmem_limit_bytes=VMEM_LIMIT)


def _dot(a, b):
    return jnp.dot(a.astype(BF16), b.astype(BF16), preferred_element_type=F32)


def _dot_nt(a, b):
    return lax.dot_general(a.astype(BF16), b.astype(BF16), (((1,), (1,)), ((), ())),
                           preferred_element_type=F32)


def _dot_tn(a, b):
    return lax.dot_general(a.astype(BF16), b.astype(BF16), (((0,), (0,)), ((), ())),
                           preferred_element_type=F32)


def _split2(x):
    hi = x.astype(BF16)
    lo = (x - hi.astype(F32)).astype(BF16)
    return hi, lo


def _split3(x):
    hi = x.astype(BF16)
    r1 = x - hi.astype(F32)
    mid = r1.astype(BF16)
    lo = (r1 - mid.astype(F32)).astype(BF16)
    return hi, mid, lo


def _dot_exact_rhs(x, m_bf16):
    hi, mid, lo = _split3(x)
    d = functools.partial(jnp.dot, preferred_element_type=F32)
    return d(hi, m_bf16) + d(mid, m_bf16) + d(lo, m_bf16)


def _dot_exact_lhs(m_bf16, x):
    hi, mid, lo = _split3(x)
    d = functools.partial(jnp.dot, preferred_element_type=F32)
    return d(m_bf16, hi) + d(m_bf16, mid) + d(m_bf16, lo)


def _rms(x, g):
    return x * lax.rsqrt(jnp.mean(x * x, axis=-1, keepdims=True) + RMS_EPS) * g


def _sigmoid(x):
    return 1.0 / (1.0 + jnp.exp(-x))


def _head_ones():
    i = lax.broadcasted_iota(jnp.int32, (C_A, C_A), 0) // HEAD
    j = lax.broadcasted_iota(jnp.int32, (C_A, C_A), 1) // HEAD
    return (i == j).astype(BF16)


def _pick_tile(n, mult, cap):
    best = mult
    t = mult
    while t <= min(n, cap):
        if n % t == 0:
            best = t
        t += mult
    assert n % best == 0, (n, mult)
    return best


def _in_proj_kernel(x_ref, g_ref, wa_ref, wb_ref, pa_ref, pb_ref):
    h = _rms(x_ref[...], g_ref[...]).astype(BF16)
    pa_ref[...] = jnp.dot(h, wa_ref[...], preferred_element_type=F32)
    pb_ref[...] = jnp.dot(h, wb_ref[...], preferred_element_type=F32)


def _in_proj(x, g, w_a, w_b):
    n = x.shape[0]
    tm = _pick_tile(n, 8, ROW_TILE)
    return pl.pallas_call(
        _in_proj_kernel,
        out_shape=(jax.ShapeDtypeStruct((n, SHIFT_COLS), F32), jax.ShapeDtypeStruct((n, 2 * C_B), F32)),
        grid=(n // tm,),
        in_specs=[pl.BlockSpec((tm, D_MODEL), lambda i: (i, 0)),
                  pl.BlockSpec((1, D_MODEL), lambda i: (0, 0)),
                  pl.BlockSpec((D_MODEL, SHIFT_COLS), lambda i: (0, 0)),
                  pl.BlockSpec((D_MODEL, 2 * C_B), lambda i: (0, 0))],
        out_specs=(pl.BlockSpec((tm, SHIFT_COLS), lambda i: (i, 0)),
                   pl.BlockSpec((tm, 2 * C_B), lambda i: (i, 0))),
        compiler_params=_cparams(("parallel",)),
        name="in_proj",
    )(x, g, w_a, w_b)


def _prep_math(p, pprev, mu, w0, a0, w_wa, w_g, k_k, k_a, ones_h, outs):
    r_ref, k_ref, v_ref, kk_ref, b_ref, lw_ref, g_ref = outs
    xs = p + (pprev - p) * mu
    r = xs[:, 0:C_A]
    k = xs[:, C_A:2 * C_A]
    v = xs[:, 2 * C_A:3 * C_A]
    lo = xs[:, 3 * C_A:3 * C_A + LORA_WA]
    g_lo = xs[:, 3 * C_A + LORA_WA:]
    lane = lax.broadcasted_iota(jnp.int32, lo.shape, 1)
    lo = jnp.where(lane < D_DECAY_LORA, jnp.tanh(lo), lo)
    wa = _dot(lo, w_wa)
    z = -(w0 + wa[:, :C_A])
    w = -(jnp.maximum(z, 0.0) + jnp.log(1.0 + jnp.exp(-jnp.abs(z)))) - 0.5
    lw_ref[...] = -jnp.exp(w)
    a = _sigmoid(a0 + wa[:, C_A:])
    g_ref[...] = _dot(_sigmoid(g_lo), w_g)
    kk = k * k_k
    hi, lo2 = _split2(kk * kk)
    ss = jnp.dot(hi, ones_h, preferred_element_type=F32) + jnp.dot(lo2, ones_h, preferred_element_type=F32)
    kk = kk / jnp.maximum(jnp.sqrt(ss), 1e-12)
    r_ref[...] = r
    v_ref[...] = v
    k_ref[...] = k * (1.0 + (a - 1.0) * k_a)
    kk_ref[...] = kk
    b_ref[...] = kk * a


def _prep_prompt_kernel(p_ref, halo_ref, mu_ref, w0_ref, a0_ref, wwa_ref, wg_ref, kk_ref_w, ka_ref, *outs):
    p = p_ref[...]
    prev_last = jnp.where(pl.program_id(1) == 0, 0.0, halo_ref[7:8, :])
    rolled = pltpu.roll(p, 1, 0)
    row = lax.broadcasted_iota(jnp.int32, p.shape, 0)
    pprev = jnp.where(row == 0, prev_last, rolled)
    _prep_math(p, pprev, mu_ref[...], w0_ref[...], a0_ref[...], wwa_ref[...], wg_ref[...],
               kk_ref_w[...], ka_ref[...], _head_ones(), outs)


def _prep_sample_kernel(p_ref, shift_ref, mu_ref, w0_ref, a0_ref, wwa_ref, wg_ref, kk_ref_w, ka_ref,
                        *outs_and_scratch):
    outs = outs_and_scratch
    p = p_ref[...]
    sb = shift_ref.shape[0]
    t_len = p.shape[0] // sb
    first = jnp.broadcast_to(shift_ref[...][:, None, :], (sb, t_len, SHIFT_COLS)).reshape(p.shape)
    row = lax.broadcasted_iota(jnp.int32, p.shape, 0)
    pprev = jnp.where(row % t_len == 0, first, pltpu.roll(p, 1, 0))
    _prep_math(p, pprev, mu_ref[...], w0_ref[...], a0_ref[...], wwa_ref[...], wg_ref[...],
               kk_ref_w[...], ka_ref[...], _head_ones(), outs)


def _prep_weight_specs(nd):
    z = (lambda *_: (0, 0))
    del nd
    return [pl.BlockSpec((1, SHIFT_COLS), z), pl.BlockSpec((1, C_A), z), pl.BlockSpec((1, C_A), z),
            pl.BlockSpec((LORA_WA, 2 * C_A), z), pl.BlockSpec((D_GATE_LORA, C_A), z),
            pl.BlockSpec((1, C_A), z), pl.BlockSpec((1, C_A), z)]


def _rwkv_prep_prompt(p_a, row0, n_seq, t_len, wts):
    tm = _pick_tile(t_len, 8, 704)
    nt = t_len // tm
    assert row0 % tm == 0
    base = row0 // tm
    out = jax.ShapeDtypeStruct((n_seq * t_len, C_A), F32)
    ospec = pl.BlockSpec((tm, C_A), lambda b, c: (b * nt + c, 0))
    return pl.pallas_call(
        _prep_prompt_kernel,
        out_shape=(out,) * 7,
        grid=(n_seq, nt),
        in_specs=[pl.BlockSpec((tm, SHIFT_COLS), lambda b, c: (base + b * nt + c, 0)),
                  pl.BlockSpec((8, SHIFT_COLS),
                               lambda b, c: (jnp.maximum((base + b * nt + c) * (tm // 8) - 1, 0), 0)),
                  ] + _prep_weight_specs(2),
        out_specs=(ospec,) * 7,
        compiler_params=_cparams(("parallel", "parallel")),
        name="rwkv_prep_prompt",
    )(p_a, p_a, *wts)


def _rwkv_prep_sample(p_a, row0, n_seq, t_len, shift, wts):
    sb = _pick_tile(n_seq, 8, 64)
    tm = sb * t_len
    assert row0 % tm == 0 and t_len % 8 == 0
    base = row0 // tm
    out = jax.ShapeDtypeStruct((n_seq * t_len, C_A), F32)
    ospec = pl.BlockSpec((tm, C_A), lambda i: (i, 0))
    return pl.pallas_call(
        _prep_sample_kernel,
        out_shape=(out,) * 7,
        grid=(n_seq // sb,),
        in_specs=[pl.BlockSpec((tm, SHIFT_COLS), lambda i: (base + i, 0)),
                  pl.BlockSpec((sb, SHIFT_COLS), lambda i: (i, 0)),
                  ] + _prep_weight_specs(1),
        out_specs=(ospec,) * 7,
        compiler_params=_cparams(("parallel",)),
        name="rwkv_prep_sample",
    )(p_a, shift, *wts)


def _scan_kernel(has_s0, unroll, *refs):
    if has_s0:
        (r_ref, k_ref, v_ref, kk_ref, b_ref, lw_ref, g_ref, s0_ref,
         lnw_ref, lnb_ref, rk_ref, y_ref, s_ref) = refs
    else:
        (r_ref, k_ref, v_ref, kk_ref, b_ref, lw_ref, g_ref,
         lnw_ref, lnb_ref, rk_ref, y_ref, s_ref) = refs
        s0_ref = None
    n_seq, c_len, _ = r_ref.shape

    @pl.when(pl.program_id(1) == 0)
    def _():
        if has_s0:
            s_ref[...] = s0_ref[...]
        else:
            s_ref[...] = jnp.zeros_like(s_ref)

    ti = lax.broadcasted_iota(jnp.int32, (c_len, c_len), 0)
    si = lax.broadcasted_iota(jnp.int32, (c_len, c_len), 1)
    tri_incl = (si <= ti).astype(BF16)
    strict = (si < ti).astype(F32)
    incl = (si <= ti).astype(F32)
    incl_signed = jnp.concatenate([incl, -incl], axis=1)
    ones_h = _head_ones()
    lnw, lnb, rk = lnw_ref[...], lnb_ref[...], rk_ref[...]
    heads = range(N_HEADS)
    hsl = [slice(h * HEAD, (h + 1) * HEAD) for h in heads]

    def gsum(x):
        hi, lo = _split2(x)
        return jnp.dot(hi, ones_h, preferred_element_type=F32) + jnp.dot(lo, ones_h, preferred_element_type=F32)

    def load(s):
        return ([ref[s] for ref in (r_ref, k_ref, v_ref, kk_ref, b_ref, lw_ref, g_ref)],
                [s_ref[s, h] for h in heads])

    def compute(streams, states):
        r, k, v, kk, b, lw, g = streams
        cum = _dot_exact_lhs(tri_incl, lw)
        tot = cum[c_len - 1:c_len, :]
        g_inv = jnp.exp(-cum)
        g_end = jnp.exp(tot - cum)
        l_mat = jnp.concatenate([kk * jnp.exp(cum - lw), r * jnp.exp(cum)], axis=0).astype(BF16)
        kbh = jnp.concatenate([k * g_inv, b * g_inv], axis=0).astype(BF16)
        e_mat = jnp.concatenate([k * g_end, -(b * g_end)], axis=0).astype(BF16)
        g_tot = jnp.exp(tot)
        l_h = [l_mat[:, hs] for hs in hsl]
        v_h = [v[:, hs] for hs in hsl]
        n_ub = [_dot_nt(l_h[h][:c_len], kbh[c_len:, hsl[h]]) * strict for h in heads]
        a_vk = [_dot_nt(l_h[h][:c_len], kbh[:c_len, hsl[h]]) * strict for h in heads]
        pm = [_dot_nt(l_h[h], states[h]) for h in heads]
        x = [pm[h][:c_len] + _dot(a_vk[h], v_h[h]) for h in heads]
        x = [x[h] - _dot(n_ub[h], x[h]) for h in heads]
        pw = n_ub
        m = 2
        while m < c_len:
            pw = [_dot(p, p) for p in pw]
            x = [x[h] + _dot(pw[h], x[h]) for h in heads]
            m *= 2
        b_m = [_dot_nt(l_h[h][c_len:], kbh[:, hsl[h]]) * incl_signed for h in heads]
        vu = [jnp.concatenate([v_h[h], x[h]], axis=0) for h in heads]
        y = jnp.concatenate([pm[h][c_len:] + _dot(b_m[h], vu[h]) for h in heads], axis=1)
        new_states = [states[h] * g_tot[:, hsl[h]] + _dot_tn(vu[h], e_mat[:, hsl[h]]) for h in heads]
        mu = gsum(y) * (1.0 / HEAD)
        d = y - mu
        var = gsum(d * d) * (1.0 / HEAD)
        yn = d * lax.rsqrt(var + GN_EPS) * lnw + lnb
        bonus = gsum(r * k * rk) * v
        return (yn + bonus) * g, new_states

    def group(base):
        idx = [base + i for i in range(unroll)]
        loaded = [load(s) for s in idx]
        results = [compute(*d) for d in loaded]
        for s, (y, new_states) in zip(idx, results):
            y_ref[s] = y
            for h in heads:
                s_ref[s, h] = new_states[h]

    if n_seq == unroll:
        group(0)
    else:
        def body(i, carry):
            group(i * unroll)
            return carry
        lax.fori_loop(0, n_seq // unroll, body, 0)


def _wkv_scan(streams, s0, n_seq, t_len, c_len, sb, unroll, lnw, lnb, rk):
    nc = t_len // c_len
    streams = [a.reshape(n_seq, t_len, C_A) for a in streams]
    dspec = pl.BlockSpec((sb, c_len, C_A), lambda i, c: (i, c, 0))
    sspec = pl.BlockSpec((sb, N_HEADS, HEAD, HEAD), lambda i, c: (i, 0, 0, 0))
    wspec = pl.BlockSpec((1, C_A), lambda i, c: (0, 0))
    ins = list(streams) + ([s0] if s0 is not None else []) + [lnw, lnb, rk]
    in_specs = [dspec] * 7 + ([sspec] if s0 is not None else []) + [wspec] * 3
    y, s_new = pl.pallas_call(
        functools.partial(_scan_kernel, s0 is not None, unroll),
        out_shape=(jax.ShapeDtypeStruct((n_seq, t_len, C_A), F32),
                   jax.ShapeDtypeStruct((n_seq, N_HEADS, HEAD, HEAD), F32)),
        grid=(n_seq // sb, nc),
        in_specs=in_specs,
        out_specs=(dspec, sspec),
        compiler_params=_cparams(("parallel", "arbitrary")),
        name="wkv_scan_c%d" % c_len,
    )(*ins)
    return y.reshape(n_seq * t_len, C_A), s_new


def _conv_tail(z, dwb, lnw, lnb):
    z = z + dwb
    mu = jnp.mean(z, axis=-1, keepdims=True)
    d = z - mu
    var = jnp.mean(d * d, axis=-1, keepdims=True)
    z = d * lax.rsqrt(var + LN_EPS) * lnw + lnb
    return z * _sigmoid(z)


def _glu(pb, glu_b):
    u = pb + glu_b
    return u[:, :C_B] * _sigmoid(u[:, C_B:])


def _conv_prompt_kernel(n_pad, pb_ref, halo_ref, glub_ref, dw_ref, dwb_ref, lnw_ref, lnb_ref,
                        y_ref, st_ref, ext_ref):
    c = pl.program_id(1)
    tm = pb_ref.shape[0]
    glub = glub_ref[...]
    halo = jnp.where(c == 0, 0.0, _glu(halo_ref[...], glub))
    u = _glu(pb_ref[...], glub)
    t_glob = c * tm + lax.broadcasted_iota(jnp.int32, u.shape, 0)
    u = jnp.where(t_glob < n_pad, 0.0, u)
    ext_ref[0:CONV_HALO, :] = halo
    ext_ref[CONV_HALO:, :] = u
    off = CONV_HALO - (CONV_W - 1)
    rb = 64

    def blk(i, carry):
        base = pl.multiple_of(i * rb, rb)
        win = ext_ref[pl.ds(base, rb + CONV_HALO), :]
        acc = jnp.zeros((rb, C_B), F32)
        for s in range(8):
            taps = [j for j in range(CONV_W) if (off + j) % 8 == s]
            shifted = win if s == 0 else pltpu.roll(win, win.shape[0] - s, 0)
            for j in taps:
                a = off + j - s
                acc = acc + shifted[a:a + rb] * dw_ref[j:j + 1, :]
        y_ref[pl.ds(base, rb), :] = _conv_tail(acc, dwb_ref[...], lnw_ref[...], lnb_ref[...])
        return carry

    lax.fori_loop(0, tm // rb, blk, 0)

    @pl.when(c == pl.num_programs(1) - 1)
    def _():
        st_ref[0] = ext_ref[CONV_HALO + tm - (CONV_W - 1):CONV_HALO + tm, :]


def _conv_prompt(p_b, row0, n_seq, t_len, n_pad, wts):
    tm = _pick_tile(t_len, CONV_HALO, 704)
    nt = t_len // tm
    assert row0 % tm == 0
    base = row0 // tm
    z2 = lambda b, c: (0, 0)
    return pl.pallas_call(
        functools.partial(_conv_prompt_kernel, n_pad),
        out_shape=(jax.ShapeDtypeStruct((n_seq * t_len, C_B), F32),
                   jax.ShapeDtypeStruct((n_seq, CONV_W - 1, C_B), F32)),
        grid=(n_seq, nt),
        in_specs=[pl.BlockSpec((tm, 2 * C_B), lambda b, c: (base + b * nt + c, 0)),
                  pl.BlockSpec((CONV_HALO, 2 * C_B),
                               lambda b, c: (jnp.maximum((base + b * nt + c) * (tm // CONV_HALO) - 1, 0), 0)),
                  pl.BlockSpec((1, 2 * C_B), z2), pl.BlockSpec((CONV_W, C_B), z2),
                  pl.BlockSpec((1, C_B), z2), pl.BlockSpec((1, C_B), z2), pl.BlockSpec((1, C_B), z2)],
        out_specs=(pl.BlockSpec((tm, C_B), lambda b, c: (b * nt + c, 0)),
                   pl.BlockSpec((1, CONV_W - 1, C_B), lambda b, c: (b, 0, 0))),
        scratch_shapes=[pltpu.VMEM((CONV_HALO + tm, C_B), F32)],
        compiler_params=_cparams(("parallel", "arbitrary")),
        name="conv_prompt",
    )(p_b, p_b, *wts)


def _conv_sample_kernel(pb_ref, buf_ref, glub_ref, dw_ref, dwb_ref, lnw_ref, lnb_ref,
                        y_ref, st_ref, ext_ref):
    sb, t_len = buf_ref.shape[0], pb_ref.shape[0] // buf_ref.shape[0]
    hist = CONV_W - 1
    u = _glu(pb_ref[...], glub_ref[...])
    lead = ext_ref.shape[1] - hist - t_len
    ext_ref[:, lead:lead + hist, :] = buf_ref[...]
    ext_ref[:, lead + hist:, :] = u.reshape(sb, t_len, C_B)
    acc = jnp.zeros((sb, t_len, C_B), F32)
    for j in range(CONV_W):
        acc = acc + ext_ref[:, lead + j:lead + j + t_len, :] * dw_ref[j:j + 1, :]
    z = _conv_tail(acc.reshape(sb * t_len, C_B), dwb_ref[...], lnw_ref[...], lnb_ref[...])
    y_ref[...] = z
    st_ref[...] = ext_ref[:, lead + t_len:, :]


def _conv_sample(p_b, row0, n_seq, t_len, buf, wts):
    sb = _pick_tile(n_seq, 8, 32)
    tm = sb * t_len
    assert row0 % tm == 0 and t_len % 8 == 0
    base = row0 // tm
    hist = CONV_W - 1
    ext_rows = -(-(hist + t_len) // 8) * 8
    z1 = lambda i: (0, 0)
    return pl.pallas_call(
        _conv_sample_kernel,
        out_shape=(jax.ShapeDtypeStruct((n_seq * t_len, C_B), F32),
                   jax.ShapeDtypeStruct((n_seq, hist, C_B), F32)),
        grid=(n_seq // sb,),
        in_specs=[pl.BlockSpec((tm, 2 * C_B), lambda i: (base + i, 0)),
                  pl.BlockSpec((sb, hist, C_B), lambda i: (i, 0, 0)),
                  pl.BlockSpec((1, 2 * C_B), z1), pl.BlockSpec((CONV_W, C_B), z1),
                  pl.BlockSpec((1, C_B), z1), pl.BlockSpec((1, C_B), z1), pl.BlockSpec((1, C_B), z1)],
        out_specs=(pl.BlockSpec((tm, C_B), lambda i: (i, 0)),
                   pl.BlockSpec((sb, hist, C_B), lambda i: (i, 0, 0))),
        scratch_shapes=[pltpu.VMEM((sb, ext_rows, C_B), F32)],
        compiler_params=_cparams(("parallel",)),
        name="conv_sample",
    )(p_b, buf, *wts)


TILE_SUB = D_MODEL // LANES


def _store_token_tiles(ref, x, lead=()):
    rows = x.shape[0]
    for c in range(TILE_SUB):
        ref[lead + (pl.ds(c, rows, stride=TILE_SUB), slice(None))] = x[:, c * LANES:(c + 1) * LANES]


def _load_token_tiles(ref, rows, lead=()):
    return jnp.concatenate([ref[lead + (pl.ds(c, rows, stride=TILE_SUB), slice(None))]
                            for c in range(TILE_SUB)], axis=1)


def _out_route_kernel(n_first, yaf_ref, yas_ref, ybf_ref, ybs_ref, x_ref, woa_ref, wob_ref, g_ref,
                      wrh_ref, wrl_ref, br_ref, x1_ref, h2_ref, idx_ref, gate_ref, rank_ref, cnt_ref, run_ref):
    i = pl.program_id(0)
    first = i < n_first
    ya = jnp.where(first, yaf_ref[...], yas_ref[...])
    yb = jnp.where(first, ybf_ref[...], ybs_ref[...])
    x1 = x_ref[...] + _dot(ya, woa_ref[...]) + _dot(yb, wob_ref[...])
    x1_ref[...] = x1
    h2 = _rms(x1, g_ref[...])
    _store_token_tiles(h2_ref, h2)
    hi, lo = _split2(h2)
    d = functools.partial(jnp.dot, preferred_element_type=F32)
    logits = d(hi, wrh_ref[...]) + d(hi, wrl_ref[...]) + d(lo, wrh_ref[...]) + br_ref[...]
    lane = lax.broadcasted_iota(jnp.int32, logits.shape, 1)
    logits = jnp.where(lane < N_EXPERTS, logits, -jnp.inf)
    idx_out = jnp.zeros(logits.shape, jnp.int32)
    val_out = jnp.full(logits.shape, -jnp.inf, F32)
    onehots = []
    for kk in range(TOP_K):
        m = jnp.max(logits, axis=-1, keepdims=True)
        sel = jnp.min(jnp.where(logits == m, lane, LANES), axis=-1, keepdims=True)
        idx_out = jnp.where(lane == kk, sel, idx_out)
        val_out = jnp.where(lane == kk, m, val_out)
        hit = lane == sel
        onehots.append(hit.astype(F32))
        logits = jnp.where(hit, -jnp.inf, logits)
    e = jnp.exp(val_out - jnp.max(val_out, axis=-1, keepdims=True))
    idx_ref[...] = idx_out
    gate_ref[...] = e / jnp.sum(e, axis=-1, keepdims=True)

    @pl.when(i == 0)
    def _():
        run_ref[...] = jnp.zeros_like(run_ref)

    tm = logits.shape[0]
    total = onehots[0] + onehots[1] + onehots[2] + onehots[3]
    ti = lax.broadcasted_iota(jnp.int32, (tm, tm), 0)
    si = lax.broadcasted_iota(jnp.int32, (tm, tm), 1)
    before = jnp.dot((si < ti).astype(BF16), total.astype(BF16), preferred_element_type=F32) + run_ref[...]
    rank = jnp.zeros(logits.shape, F32)
    for kk in range(TOP_K):
        rank = jnp.where(lane == kk, jnp.sum(onehots[kk] * before, axis=-1, keepdims=True), rank)
    rank_ref[...] = rank.astype(jnp.int32)
    run_ref[...] += jnp.sum(total, axis=0, keepdims=True)
    cnt_ref[...] = run_ref[...]


def _out_route(ya_f, ya_s, yb_f, yb_s, x, wo_a, wo_b, g, wr_hi, wr_lo, b_r):
    n = x.shape[0]
    n_f = ya_f.shape[0]
    tm = ROW_TILE
    while n_f % tm or (n - n_f) % tm:
        tm //= 2
    assert tm % 8 == 0
    nf_t, ns_t = n_f // tm, (n - n_f) // tm
    z = lambda i: (0, 0)
    row = lambda w: pl.BlockSpec((tm, w), lambda i: (i, 0))
    fst = lambda w: pl.BlockSpec((tm, w), lambda i: (jnp.minimum(i, nf_t - 1), 0))
    snd = lambda w: pl.BlockSpec((tm, w), lambda i: (jnp.maximum(i - nf_t, 0), 0))
    return pl.pallas_call(
        functools.partial(_out_route_kernel, nf_t),
        out_shape=(jax.ShapeDtypeStruct((n, D_MODEL), F32), jax.ShapeDtypeStruct((n * TILE_SUB, LANES), F32),
                   jax.ShapeDtypeStruct((n, LANES), jnp.int32), jax.ShapeDtypeStruct((n, LANES), F32),
                   jax.ShapeDtypeStruct((n, LANES), jnp.int32), jax.ShapeDtypeStruct((1, LANES), F32)),
        grid=(nf_t + ns_t,),
        in_specs=[fst(C_A), snd(C_A), fst(C_B), snd(C_B), row(D_MODEL),
                  pl.BlockSpec((C_A, D_MODEL), z), pl.BlockSpec((C_B, D_MODEL), z),
                  pl.BlockSpec((1, D_MODEL), z),
                  pl.BlockSpec((D_MODEL, LANES), z), pl.BlockSpec((D_MODEL, LANES), z),
                  pl.BlockSpec((1, LANES), z)],
        out_specs=(row(D_MODEL), pl.BlockSpec((tm * TILE_SUB, LANES), lambda i: (i, 0)), row(LANES), row(LANES),
                   row(LANES), pl.BlockSpec((1, LANES), z)),
        scratch_shapes=[pltpu.VMEM((1, LANES), F32)],
        compiler_params=_cparams(("arbitrary",)),
        name="out_route",
    )(ya_f, ya_s, yb_f, yb_s, x, wo_a, wo_b, g, wr_hi, wr_lo, b_r)


FFN_BUFS = 3


def _ffn_kernel(n_tok, asg_ref, bexp_ref, nblk_ref, h2_hbm, wg_ref, bg_ref, wu_ref, bu_ref, wd_ref, bd_ref,
                out_hbm, *scratch):
    xbufs, ybufs = scratch[:FFN_BUFS], scratch[FFN_BUFS:2 * FFN_BUFS]
    wg_bf, wu_bf, wd_bf, gsem, ssem = scratch[2 * FFN_BUFS:]
    j = pl.program_id(0)
    last = pl.num_programs(0) - 1
    n_used = nblk_ref[0]
    rows = MOE_ROWS

    def gather(i, tab_row, s):
        tok = jnp.minimum(lax.shift_right_logical(asg_ref[tab_row, i], jnp.int32(2)), n_tok - 1)
        return pltpu.make_async_copy(h2_hbm.at[pl.ds(pl.multiple_of(tok * TILE_SUB, TILE_SUB), TILE_SUB), :],
                                     xbufs[s].at[pl.ds(i * TILE_SUB, TILE_SUB), :], gsem.at[s])

    def scatter(i, tab_row, s):
        dst = asg_ref[tab_row, i]
        return pltpu.make_async_copy(ybufs[s].at[pl.ds(i * TILE_SUB, TILE_SUB), :],
                                     out_hbm.at[pl.ds(pl.multiple_of(dst * TILE_SUB, TILE_SUB), TILE_SUB), :],
                                     ssem.at[s])

    def start_all(fn, unrolled):
        if unrolled:
            for i in range(rows):
                fn(i).start(priority=i % 2)
        else:
            def body(i, carry):
                fn(i).start()
                return carry
            lax.fori_loop(0, rows, body, 0)

    def wait_gather(s):
        pltpu.make_async_copy(h2_hbm.at[pl.ds(0, rows * TILE_SUB), :], xbufs[s], gsem.at[s]).wait()

    def wait_scatter(s):
        pltpu.make_async_copy(ybufs[s], out_hbm.at[pl.ds(0, rows * TILE_SUB), :], ssem.at[s]).wait()

    @pl.when(j == 0)
    def _():
        for yb in ybufs:
            yb[...] = jnp.zeros_like(yb)
        start_all(lambda i: gather(i, 1, 0), False)
        start_all(lambda i: gather(i, 2, 1), False)

    def step(slot):
        prev = (slot + FFN_BUFS - 1) % FFN_BUFS
        nxt = (slot + 1) % FFN_BUFS

        def start_next(unrolled):
            start_all(lambda i: gather(i, j + 3, prev), unrolled)
            start_all(lambda i: scatter(i, j, prev), unrolled)

        @pl.when(j >= 2)
        def _():
            wait_scatter(slot)

        wait_gather(slot)

        @pl.when(j < n_used)
        def _():
            new_expert = jnp.logical_or(j == 0, bexp_ref[j] != bexp_ref[jnp.maximum(j - 1, 0)])

            @pl.when(new_expert)
            def _():
                wg_bf[...] = wg_ref[0].astype(BF16)
                wu_bf[...] = wu_ref[0].astype(BF16)
                wd_bf[...] = wd_ref[0].astype(BF16)

            x = _load_token_tiles(xbufs[slot], rows).astype(BF16)
            start_next(True)
            gt = jnp.minimum(jnp.dot(x, wg_bf[...], preferred_element_type=F32) + bg_ref[0], SWIGLU_LIMIT)
            up = jnp.clip(jnp.dot(x, wu_bf[...], preferred_element_type=F32) + bu_ref[0],
                          -SWIGLU_LIMIT, SWIGLU_LIMIT)
            act = (up + 1.0) * gt * _sigmoid(SWIGLU_ALPHA * gt)
            y = jnp.dot(act.astype(BF16), wd_bf[...], preferred_element_type=F32) + bd_ref[0]
            _store_token_tiles(ybufs[slot], y)

        @pl.when(j >= n_used)
        def _():
            start_next(False)

        @pl.when(j == last)
        def _():
            for s in (nxt, prev):
                wait_gather(s)
                wait_scatter(s)

    for s in range(FFN_BUFS):
        pl.when(lax.rem(j, FFN_BUFS) == s)(functools.partial(step, s))


def _moe_ffn(asg, block_expert, n_used, h2t, w_gate, b_gate, w_up, b_up, w_down, b_down):
    n_steps = block_expert.shape[0]
    n_tok = h2t.shape[0] // TILE_SUB
    d_ff = w_gate.shape[2]
    out_rows = n_steps * MOE_ROWS
    wspec = lambda shape: pl.BlockSpec((1,) + shape, lambda j, a, be, nb: (be[j], 0, 0))
    tiles = (MOE_ROWS * TILE_SUB, LANES)
    return pl.pallas_call(
        functools.partial(_ffn_kernel, n_tok),
        out_shape=jax.ShapeDtypeStruct((out_rows * TILE_SUB, LANES), F32),
        grid_spec=pltpu.PrefetchScalarGridSpec(
            num_scalar_prefetch=3,
            grid=(n_steps,),
            in_specs=[pl.BlockSpec(memory_space=pl.ANY),
                      wspec((D_MODEL, d_ff)), wspec((1, d_ff)),
                      wspec((D_MODEL, d_ff)), wspec((1, d_ff)),
                      wspec((d_ff, D_MODEL)), wspec((1, D_MODEL))],
            out_specs=pl.BlockSpec(memory_space=pl.ANY),
            scratch_shapes=[pltpu.VMEM(tiles, F32)] * (2 * FFN_BUFS) + [
                pltpu.VMEM((D_MODEL, d_ff), BF16), pltpu.VMEM((D_MODEL, d_ff), BF16),
                pltpu.VMEM((d_ff, D_MODEL), BF16),
                pltpu.SemaphoreType.DMA((FFN_BUFS,)), pltpu.SemaphoreType.DMA((FFN_BUFS,))]),
        compiler_params=_cparams(("arbitrary",)),
        name="moe_ffn",
    )(asg, block_expert, n_used, h2t, w_gate, b_gate.reshape(N_EXPERTS, 1, d_ff),
      w_up, b_up.reshape(N_EXPERTS, 1, d_ff), w_down, b_down.reshape(N_EXPERTS, 1, D_MODEL))


def _combine_kernel(e_ref, x1_ref, gate_ref, g_ref, y_ref):
    tm = x1_ref.shape[0]
    gates = gate_ref[...]
    acc = x1_ref[...]
    for kk in range(TOP_K):
        yk = jnp.concatenate([e_ref[pl.ds(kk * TILE_SUB + c, tm, stride=TOP_K * TILE_SUB), :]
                              for c in range(TILE_SUB)], axis=1)
        acc = acc + yk * gates[:, kk:kk + 1]
    y_ref[...] = _rms(acc, g_ref[...])


def _combine(yt, x1, gates, g_final):
    n = x1.shape[0]
    tm = _pick_tile(n, 8, 256)
    return pl.pallas_call(
        _combine_kernel,
        out_shape=jax.ShapeDtypeStruct((n, D_MODEL), F32),
        grid=(n // tm,),
        in_specs=[pl.BlockSpec((tm * TOP_K * TILE_SUB, LANES), lambda i: (i, 0)),
                  pl.BlockSpec((tm, D_MODEL), lambda i: (i, 0)),
                  pl.BlockSpec((tm, LANES), lambda i: (i, 0)),
                  pl.BlockSpec((1, D_MODEL), lambda i: (0, 0))],
        out_specs=pl.BlockSpec((tm, D_MODEL), lambda i: (i, 0)),
        compiler_params=_cparams(("parallel",)),
        name="moe_combine",
    )(yt, x1, gates, g_final)


def _routing(idx, rank, counts):
    n = idx.shape[0]
    n_assign = n * TOP_K
    flat_e = idx[:, :TOP_K].reshape(-1)
    rank = rank[:, :TOP_K].reshape(-1)
    counts = counts[0, :N_EXPERTS].astype(jnp.int32)
    padded = (counts + MOE_ROWS - 1) // MOE_ROWS * MOE_ROWS
    pad_end = jnp.cumsum(padded)
    pad_start = pad_end - padded
    n_blocks = (n_assign + N_EXPERTS * (MOE_ROWS - 1) + MOE_ROWS - 1) // MOE_ROWS
    dest = (pad_start[flat_e] + rank).astype(jnp.int32) + MOE_ROWS
    tab_rows = n_blocks + 1 + FFN_BUFS
    q = jnp.arange(tab_rows * MOE_ROWS, dtype=jnp.int32) - MOE_ROWS
    e_blk = jnp.sum((q[::MOE_ROWS, None] >= pad_end[None, :]).astype(jnp.int32), axis=1)
    e_of = jnp.broadcast_to(e_blk[:, None], (tab_rows, MOE_ROWS)).reshape(-1)
    waste = padded - counts
    waste_before = jnp.concatenate([jnp.zeros((1,), waste.dtype), jnp.cumsum(waste)])
    e_c = jnp.minimum(e_of, N_EXPERTS - 1)
    in_expert = MOE_ROWS + waste_before[e_c] + (q - pad_start[e_c] - counts[e_c])
    after = MOE_ROWS + waste_before[N_EXPERTS] + (q - pad_end[N_EXPERTS - 1])
    dump = jnp.where(q < 0, q + MOE_ROWS, jnp.where(e_of < N_EXPERTS, in_expert, after))
    asg = (n_assign + dump).astype(jnp.int32).at[dest].set(
        jnp.arange(n_assign, dtype=jnp.int32), unique_indices=True, mode='promise_in_bounds')
    n_used = (pad_end[-1] // MOE_ROWS).astype(jnp.int32)
    blk_start = jnp.arange(n_blocks + 1, dtype=jnp.int32) * MOE_ROWS
    block_expert = jnp.minimum(jnp.sum((blk_start[:, None] >= pad_end[None, :]).astype(jnp.int32), axis=1),
                               N_EXPERTS - 1)
    last_used = block_expert[jnp.maximum(n_used - 1, 0)]
    block_expert = jnp.where(jnp.arange(n_blocks + 1) < n_used, block_expert, last_used).astype(jnp.int32)
    return asg.reshape(tab_rows, MOE_ROWS), block_expert, n_used.reshape(1)


def _forward(x_prompt, x_sample, state_wkv, state_shift, state_conv, meta_tokens, norm_mix, w_in,
             tshift_mu, decay_w0, decay_w2, iclr_a0, iclr_a2, gate_g2, k_k, k_a, r_k, lnx_w, lnx_b,
             glu_b, dw_weight, dw_bias, conv_ln_w, conv_ln_b, w_out, norm_ffn, w_router, b_router,
             w_gate, b_gate, w_up, b_up, w_down, b_down, norm_final):
    n_p, seq, _ = x_prompt.shape
    n_s, t_s, _ = x_sample.shape
    t_real = N_META + seq
    n_pad = (-t_real) % CHUNK
    t_p = t_real + n_pad
    rows_p = n_p * t_p
    rows_s = n_s * t_s
    lyr = 0

    lead = jnp.concatenate([jnp.zeros((n_pad, D_MODEL), F32), meta_tokens.astype(F32)], axis=0)
    pieces = []
    for b in range(n_p):
        pieces += [lead, x_prompt[b]]
    x_all = jnp.concatenate(pieces + [x_sample.reshape(rows_s, D_MODEL)], axis=0)

    row = lambda a: a[lyr].reshape(1, -1).astype(F32)
    w_in_bf = w_in[lyr].astype(BF16)
    p_a, p_b = _in_proj(x_all, row(norm_mix), w_in_bf[:, :SHIFT_COLS], w_in_bf[:, SHIFT_COLS:])

    w_wa = jnp.zeros((LORA_WA, 2 * C_A), F32)
    w_wa = w_wa.at[:D_DECAY_LORA, :C_A].set(decay_w2[lyr]).at[D_DECAY_LORA:, C_A:].set(iclr_a2[lyr])
    prep_w = (row(tshift_mu), row(decay_w0), row(iclr_a0), w_wa.astype(BF16), gate_g2[lyr].astype(BF16),
              row(k_k), row(k_a))
    st_p = _rwkv_prep_prompt(p_a, 0, n_p, t_p, prep_w)
    st_s = _rwkv_prep_sample(p_a, rows_p, n_s, t_s, state_shift[lyr], prep_w)

    scan_w = (row(lnx_w), row(lnx_b), row(r_k))
    sb_p = _pick_tile(n_p, 1, SCAN_UNROLL)
    ya_p, wkv_p = _wkv_scan(st_p, None, n_p, t_p, CHUNK, sb_p, sb_p, *scan_w)
    sb_s = _pick_tile(n_s, 1, 16)
    ya_s, wkv_s = _wkv_scan(st_s, state_wkv[lyr], n_s, t_s, t_s, sb_s, _pick_tile(sb_s, 1, SCAN_UNROLL), *scan_w)

    conv_w = (row(glu_b), dw_weight[lyr], row(dw_bias), row(conv_ln_w), row(conv_ln_b))
    yb_p, conv_p = _conv_prompt(p_b, 0, n_p, t_p, n_pad, conv_w)
    yb_s, conv_s = _conv_sample(p_b, rows_p, n_s, t_s, state_conv[lyr], conv_w)

    w_out_bf = w_out[lyr].astype(BF16)
    wr = jnp.zeros((D_MODEL, LANES), F32).at[:, :N_EXPERTS].set(w_router[lyr])
    wr_hi = wr.astype(BF16)
    wr_lo = (wr - wr_hi.astype(F32)).astype(BF16)
    b_r = jnp.zeros((1, LANES), F32).at[0, :N_EXPERTS].set(b_router[lyr])
    x1, h2, idx, gates, rank, counts = _out_route(ya_p, ya_s, yb_p, yb_s, x_all, w_out_bf[:C_A], w_out_bf[C_A:],
                                                  row(norm_ffn), wr_hi, wr_lo, b_r)

    asg, block_expert, n_used = _routing(idx, rank, counts)
    yt = _moe_ffn(asg, block_expert, n_used, h2, w_gate[lyr], b_gate[lyr], w_up[lyr], b_up[lyr],
                  w_down[lyr], b_down[lyr])
    y = _combine(yt, x1, gates, norm_final.reshape(1, -1))

    y_prompt = y[:rows_p].reshape(n_p, t_p, D_MODEL)[:, n_pad + N_META:]
    y_sample = y[rows_p:].reshape(n_s, t_s, D_MODEL)
    shift_p = p_a[t_p - 1:rows_p:t_p]
    shift_s = p_a[rows_p + t_s - 1::t_s]
    return (y_prompt.astype(x_prompt.dtype), y_sample.astype(x_sample.dtype),
            wkv_p[None], shift_p[None], conv_p[None], wkv_s[None], shift_s[None], conv_s[None])


def kernel(x_prompt, x_sample, state_wkv, state_shift, state_conv, meta_tokens, norm_mix, w_in, tshift_mu, decay_w0, decay_w2, iclr_a0, iclr_a2, gate_g2, k_k, k_a, r_k, lnx_w, lnx_b, glu_b, dw_weight, dw_bias, conv_ln_w, conv_ln_b, w_out, norm_ffn, w_router, b_router, w_gate, b_gate, w_up, b_up, w_down, b_down, norm_final):
    assert w_in.shape[0] == 1, "single trunk layer"
    return _forward(x_prompt, x_sample, state_wkv, state_shift, state_conv, meta_tokens, norm_mix, w_in,
                    tshift_mu, decay_w0, decay_w2, iclr_a0, iclr_a2, gate_g2, k_k, k_a, r_k, lnx_w, lnx_b,
                    glu_b, dw_weight, dw_bias, conv_ln_w, conv_ln_b, w_out, norm_ffn, w_router, b_router,
                    w_gate, b_gate, w_up, b_up, w_down, b_down, norm_final)
```
